```python
import math
import jax, jax.numpy as jnp
from jax import lax
import numpy as np

D_MODEL = 2048
BATCH = 4
SEQ = 2048
DEPTH = 4

N_A_LAYERS = DEPTH // 2
N_B_LAYERS = DEPTH - N_A_LAYERS
EPS = 1e-6

GLA_HEADS = 4
GLA_DK = D_MODEL // 2
GLA_DV = D_MODEL
GLA_HK = GLA_DK // GLA_HEADS
GLA_HV = GLA_DV // GLA_HEADS
GLA_GATE_RANK = 16
GLA_GATE_TAU = 16.0
GLA_CHUNK = 64
GLA_IN = 2 * GLA_DK + 2 * GLA_DV + GLA_GATE_RANK

DIFF_HEAD_DIM = 128
DIFF_HEADS = D_MODEL // (2 * DIFF_HEAD_DIM)
DIFF_VDIM = 2 * DIFF_HEAD_DIM
DIFF_QK = DIFF_HEADS * 2 * DIFF_HEAD_DIM
DIFF_V = DIFF_HEADS * DIFF_VDIM
Q_BLOCK = 128

D_FF = -(-8 * D_MODEL // (3 * 256)) * 256

kernel_name = "yoco_gla_diffattn_hybrid"


def rms_norm(x, gain):
    xf = x.astype(jnp.float32)
    y = xf * lax.rsqrt(jnp.mean(xf * xf, axis=-1, keepdims=True) + EPS)
    return (y * gain.astype(jnp.float32)).astype(x.dtype)


def alibi_slopes(n_heads):
    return 2.0 ** (-8.0 * jnp.arange(1, n_heads + 1, dtype=jnp.float32) / n_heads)


def swiglu(h, w_in, w_out):
    g, u = jnp.split(h @ w_in, 2, axis=-1)
    return (jax.nn.silu(g) * u) @ w_out


def gla_mixer(h, w_in, w_alpha2, b_alpha, head_gain, w_out):
    B, S, _ = h.shape
    C = GLA_CHUNK
    N = S // C
    proj = h @ w_in
    q, k, v, r, a_low = jnp.split(
        proj, [GLA_DK, 2 * GLA_DK, 2 * GLA_DK + GLA_DV, 2 * GLA_DK + 2 * GLA_DV], axis=-1)
    log_a = jax.nn.log_sigmoid((a_low @ w_alpha2 + b_alpha).astype(jnp.float32)) / GLA_GATE_TAU

    def heads(t, d):
        return t.astype(jnp.float32).reshape(B, N, C, GLA_HEADS, d).transpose(0, 3, 1, 2, 4)

    qc = heads(q, GLA_HK) * (GLA_HK ** -0.5)
    kc = heads(k, GLA_HK)
    vc = heads(v, GLA_HV)
    bcum = jnp.cumsum(heads(log_a, GLA_HK), axis=3)
    b_last = bcum[:, :, :, -1:, :]
    q_dec = qc * jnp.exp(bcum)
    k_inv = kc * jnp.exp(-bcum)
    k_end = kc * jnp.exp(b_last - bcum)

    causal = jnp.tril(jnp.ones((C, C), dtype=bool))
    attn = jnp.where(causal, jnp.einsum('bhntk,bhnsk->bhnts', q_dec, k_inv), 0.0)
    o_intra = jnp.einsum('bhnts,bhnsv->bhntv', attn, vc)

    def step(state, inp):
        q_n, k_n, v_n, dec_n = inp
        o_n = jnp.einsum('bhtk,bhkv->bhtv', q_n, state)
        state = state * dec_n[..., None] + jnp.einsum('bhsk,bhsv->bhkv', k_n, v_n)
        return state, o_n

    s0 = jnp.zeros((B, GLA_HEADS, GLA_HK, GLA_HV), jnp.float32)
    xs = (jnp.moveaxis(q_dec, 2, 0), jnp.moveaxis(k_end, 2, 0), jnp.moveaxis(vc, 2, 0),
          jnp.moveaxis(jnp.exp(b_last[:, :, :, 0, :]), 2, 0))
    _, o_inter = lax.scan(step, s0, xs)
    o = o_intra + jnp.moveaxis(o_inter, 0, 2)
    o = rms_norm(o, head_gain)
    o = o.transpose(0, 2, 3, 1, 4).reshape(B, S, GLA_DV)
    o = o * jax.nn.silu(r.astype(jnp.float32))
    return o.astype(h.dtype) @ w_out


def shared_kv(z, kv_norm, w_kv, k_gain):
    B, S, _ = z.shape
    kv = rms_norm(z, kv_norm) @ w_kv
    k, v = jnp.split(kv, [DIFF_QK], axis=-1)
    k = rms_norm(k.reshape(B, S, DIFF_HEADS, 2, DIFF_HEAD_DIM), k_gain)
    k = k.transpose(0, 2, 3, 1, 4)
    v = v.reshape(B, S, DIFF_HEADS, DIFF_VDIM).transpose(0, 2, 1, 3)
    return k, v


def diff_attention(h, k, v, w_q, q_gain, lq1, lk1, lq2, lk2, head_gain, w_out, lambda_init):
    B, S, _ = h.shape
    q = rms_norm((h @ w_q).reshape(B, S, DIFF_HEADS, 2, DIFF_HEAD_DIM), q_gain)
    q = q.astype(jnp.float32).transpose(0, 2, 3, 1, 4) * (DIFF_HEAD_DIM ** -0.5)
    lam = (jnp.exp(jnp.sum(lq1.astype(jnp.float32) * lk1.astype(jnp.float32)))
           - jnp.exp(jnp.sum(lq2.astype(jnp.float32) * lk2.astype(jnp.float32)))
           + lambda_init)
    slopes = alibi_slopes(DIFF_HEADS)
    nb = S // Q_BLOCK
    qb = q.reshape(B, DIFF_HEADS, 2, nb, Q_BLOCK, DIFF_HEAD_DIM).transpose(3, 0, 1, 2, 4, 5)
    kf = k.astype(jnp.float32)
    vf = v.astype(jnp.float32)
    key_pos = jnp.arange(S)

    def block(args):
        q_blk, i = args
        q_pos = i * Q_BLOCK + jnp.arange(Q_BLOCK)
        dist = (q_pos[:, None] - key_pos[None, :]).astype(jnp.float32)
        bias = -slopes[:, None, None] * dist
        s = jnp.einsum('bhcqd,bhcsd->bhcqs', q_blk, kf) + bias[None, :, None]
        s = jnp.where(dist >= 0, s, -jnp.inf)
        p = jax.nn.softmax(s, axis=-1)
        a = p[:, :, 0] - lam * p[:, :, 1]
        return jnp.einsum('bhqs,bhsv->bhqv', a, vf)

    o = lax.map(block, (qb, jnp.arange(nb)))
    o = o.transpose(1, 2, 0, 3, 4).reshape(B, DIFF_HEADS, S, DIFF_VDIM)
    o = rms_norm(o, head_gain) * (1.0 - lambda_init)
    o = o.transpose(0, 2, 1, 3).reshape(B, S, DIFF_V).astype(h.dtype)
    return o @ w_out


def setup_inputs(seed: int = 0) -> dict:
    key = jax.random.key(seed)
    ks = jax.random.split(key, 24)
    f32 = jnp.float32

    def w(k, shape, fan_in):
        return jax.random.normal(k, shape, f32) * (fan_in ** -0.5)

    def gain(k, shape):
        return 1.0 + 0.02 * jax.random.normal(k, shape, f32)

    return {
        "x": jax.random.normal(ks[0], (BATCH, SEQ, D_MODEL), f32),
        "gla_attn_norm": gain(ks[1], (N_A_LAYERS, D_MODEL)),
        "gla_w_in": w(ks[2], (N_A_LAYERS, D_MODEL, GLA_IN), D_MODEL),
        "gla_w_alpha2": w(ks[3], (N_A_LAYERS, GLA_GATE_RANK, GLA_DK), GLA_GATE_RANK),
        "gla_b_alpha": 0.1 * jax.random.normal(ks[4], (N_A_LAYERS, GLA_DK), f32),
        "gla_head_norm": gain(ks[5], (N_A_LAYERS, GLA_HV)),
        "gla_w_out": w(ks[6], (N_A_LAYERS, GLA_DV, D_MODEL), GLA_DV),
        "kv_norm": gain(ks[7], (D_MODEL,)),
        "w_kv": w(ks[8], (D_MODEL, DIFF_QK + DIFF_V), D_MODEL),
        "k_norm": gain(ks[9], (DIFF_HEAD_DIM,)),
        "diff_attn_norm": gain(ks[10], (N_B_LAYERS, D_MODEL)),
        "diff_w_q": w(ks[11], (N_B_LAYERS, D_MODEL, DIFF_QK), D_MODEL),
        "diff_q_norm": gain(ks[12], (N_B_LAYERS, DIFF_HEAD_DIM)),
        "diff_lambda_q1": 0.1 * jax.random.normal(ks[13], (N_B_LAYERS, DIFF_HEAD_DIM), f32),
        "diff_lambda_k1": 0.1 * jax.random.normal(ks[14], (N_B_LAYERS, DIFF_HEAD_DIM), f32),
        "diff_lambda_q2": 0.1 * jax.random.normal(ks[15], (N_B_LAYERS, DIFF_HEAD_DIM), f32),
        "diff_lambda_k2": 0.1 * jax.random.normal(ks[16], (N_B_LAYERS, DIFF_HEAD_DIM), f32),
        "diff_head_norm": gain(ks[17], (N_B_LAYERS, DIFF_VDIM)),
        "diff_w_out": w(ks[18], (N_B_LAYERS, DIFF_V, D_MODEL), DIFF_V),
        "ffn_norm": gain(ks[19], (DEPTH, D_MODEL)),
        "ffn_w_in": w(ks[20], (DEPTH, D_MODEL, 2 * D_FF), D_MODEL),
        "ffn_w_out": w(ks[21], (DEPTH, D_FF, D_MODEL), D_FF),
    }


def reference(x, gla_attn_norm, gla_w_in, gla_w_alpha2, gla_b_alpha, gla_head_norm, gla_w_out,
              kv_norm, w_kv, k_norm,
              diff_attn_norm, diff_w_q, diff_q_norm, diff_lambda_q1, diff_lambda_k1,
              diff_lambda_q2, diff_lambda_k2, diff_head_norm, diff_w_out,
              ffn_norm, ffn_w_in, ffn_w_out):
    k_shared = None
    v_shared = None
    for l in range(DEPTH):
        if l < N_A_LAYERS:
            i = l
            x = x + gla_mixer(rms_norm(x, gla_attn_norm[i]), gla_w_in[i], gla_w_alpha2[i],
                              gla_b_alpha[i], gla_head_norm[i], gla_w_out[i])
        else:
            if l == N_A_LAYERS:
                k_shared, v_shared = shared_kv(x, kv_norm, w_kv, k_norm)
            j = l - N_A_LAYERS
            lambda_init = 0.8 - 0.6 * math.exp(-0.3 * l)
            x = x + diff_attention(rms_norm(x, diff_attn_norm[j]), k_shared, v_shared,
                                   diff_w_q[j], diff_q_norm[j], diff_lambda_q1[j],
                                   diff_lambda_k1[j], diff_lambda_q2[j], diff_lambda_k2[j],
                                   diff_head_norm[j], diff_w_out[j], lambda_init)
        x = x + swiglu(rms_norm(x, ffn_norm[l]), ffn_w_in[l], ffn_w_out[l])
    return x
```

```python
import functools
import math

import jax
import jax.numpy as jnp
from jax import lax
from jax.experimental import pallas as pl
from jax.experimental.pallas import tpu as pltpu

D_MODEL = 2048
BATCH = 4
SEQ = 2048
DEPTH = 4
TOKENS = BATCH * SEQ
N_A_LAYERS = DEPTH // 2
N_B_LAYERS = DEPTH - N_A_LAYERS
EPS = 1e-6

GLA_HEADS = 4
GLA_DK = D_MODEL // 2
GLA_DV = D_MODEL
GLA_HK = GLA_DK // GLA_HEADS
GLA_HV = GLA_DV // GLA_HEADS
GLA_GATE_RANK = 16
GLA_GATE_TAU = 16.0
GLA_CHUNK = 64
GLA_MAIN = 2 * GLA_DK + 2 * GLA_DV

DIFF_HEAD_DIM = 128
DIFF_HEADS = D_MODEL // (2 * DIFF_HEAD_DIM)
DIFF_VDIM = 2 * DIFF_HEAD_DIM
DIFF_QK = DIFF_HEADS * 2 * DIFF_HEAD_DIM
DIFF_V = DIFF_HEADS * DIFF_VDIM

D_FF = -(-8 * D_MODEL // (3 * 256)) * 256

LANES = 128
MASK_VALUE = -1e30

BF16 = jnp.bfloat16
F32 = jnp.float32

VMEM_LIMIT = 56 * 1024 * 1024


def _params(n_axes):
    return pltpu.CompilerParams(dimension_semantics=("arbitrary",) * n_axes,
                                vmem_limit_bytes=VMEM_LIMIT)


def _silu(x):
    return x / (1.0 + jnp.exp(-x))


NORM_ROWS = 128


def _fill_normed(x_ref, g_ref, xn_ref):
    gain = g_ref[...]

    def body(i, carry):
        rows = pl.ds(pl.multiple_of(i * NORM_ROWS, NORM_ROWS), NORM_ROWS)
        xv = x_ref[rows, :]
        ms = jnp.mean(xv * xv, axis=-1, keepdims=True)
        xn_ref[rows, :] = ((xv * lax.rsqrt(ms + EPS)) * gain).astype(BF16)
        return carry

    lax.fori_loop(0, x_ref.shape[0] // NORM_ROWS, body, 0)


def _group_norm_store(y, gn_ref, o_ref, scale):
    gain = gn_ref[...]
    for g in range(y.shape[1] // LANES):
        yg = y[:, g * LANES:(g + 1) * LANES]
        ms = jnp.mean(yg * yg, axis=-1, keepdims=True)
        val = (yg * lax.rsqrt(ms + EPS)) * gain
        if scale != 1.0:
            val = val * scale
        o_ref[:, g * LANES:(g + 1) * LANES] = val.astype(o_ref.dtype)


def _norm_proj_kernel(x_ref, g_ref, w_ref, *rest, group_norm, scale, with_gate):
    rest = list(rest)
    w1_ref = rest.pop(0) if with_gate else None
    gn_ref = rest.pop(0) if group_norm else None
    o_ref = rest.pop(0)
    a_ref = rest.pop(0) if with_gate else None
    xn_ref = rest.pop(0)

    @pl.when(pl.program_id(1) == 0)
    def _():
        _fill_normed(x_ref, g_ref, xn_ref)
        if with_gate:
            a_ref[...] = jnp.dot(xn_ref[...], w1_ref[...], preferred_element_type=F32)

    y = jnp.dot(xn_ref[...], w_ref[...], preferred_element_type=F32)
    if group_norm:
        _group_norm_store(y, gn_ref, o_ref, scale)
    else:
        o_ref[...] = y.astype(o_ref.dtype)


def _norm_proj(x, gain, w, *, bm, bn, out_dtype, w1=None, gn_gain=None, scale=1.0):
    m, k = x.shape
    n = w.shape[1]
    with_gate = w1 is not None
    group_norm = gn_gain is not None
    in_specs = [pl.BlockSpec((bm, k), lambda i, j: (i, 0)),
                pl.BlockSpec((1, k), lambda i, j: (0, 0)),
                pl.BlockSpec((k, bn), lambda i, j: (0, j))]
    args = [x, gain.reshape(1, k), w]
    if with_gate:
        in_specs.append(pl.BlockSpec((k, LANES), lambda i, j: (0, 0)))
        args.append(w1)
    if group_norm:
        in_specs.append(pl.BlockSpec((1, LANES), lambda i, j: (0, 0)))
        args.append(gn_gain.reshape(1, LANES))
    out_shape = [jax.ShapeDtypeStruct((m, n), out_dtype)]
    out_specs = [pl.BlockSpec((bm, bn), lambda i, j: (i, j))]
    if with_gate:
        out_shape.append(jax.ShapeDtypeStruct((m, LANES), F32))
        out_specs.append(pl.BlockSpec((bm, LANES), lambda i, j: (i, 0)))
    res = pl.pallas_call(
        functools.partial(_norm_proj_kernel, group_norm=group_norm, scale=scale, with_gate=with_gate),
        grid=(m // bm, n // bn),
        in_specs=in_specs,
        out_specs=out_specs,
        out_shape=out_shape,
        scratch_shapes=[pltpu.VMEM((bm, k), BF16)],
        compiler_params=_params(2),
    )(*args)
    return res if with_gate else res[0]


def _ffn_up_kernel(x_ref, g_ref, wg_ref, wu_ref, o_ref, xn_ref):
    @pl.when(pl.program_id(1) == 0)
    def _():
        _fill_normed(x_ref, g_ref, xn_ref)

    xn = xn_ref[...]
    g = jnp.dot(xn, wg_ref[...], preferred_element_type=F32)
    u = jnp.dot(xn, wu_ref[...], preferred_element_type=F32)
    o_ref[...] = (_silu(g) * u).astype(o_ref.dtype)


def _ffn_up(x, gain, w_in, *, bm, bn):
    m, k = x.shape
    nblk = D_FF // bn
    return pl.pallas_call(
        _ffn_up_kernel,
        grid=(m // bm, nblk),
        in_specs=[pl.BlockSpec((bm, k), lambda i, j: (i, 0)),
                  pl.BlockSpec((1, k), lambda i, j: (0, 0)),
                  pl.BlockSpec((k, bn), lambda i, j: (0, j)),
                  pl.BlockSpec((k, bn), lambda i, j: (0, j + nblk))],
        out_specs=pl.BlockSpec((bm, bn), lambda i, j: (i, j)),
        out_shape=jax.ShapeDtypeStruct((m, D_FF), BF16),
        scratch_shapes=[pltpu.VMEM((bm, k), BF16)],
        compiler_params=_params(2),
    )(x, gain.reshape(1, k), w_in, w_in)


def _res_mm_kernel(a_ref, w_ref, r_ref, o_ref):
    o_ref[...] = r_ref[...] + jnp.dot(a_ref[...], w_ref[...], preferred_element_type=F32)


def _res_mm(a, w, res, *, bm, bn):
    m, k = a.shape
    n = w.shape[1]
    return pl.pallas_call(
        _res_mm_kernel,
        grid=(m // bm, n // bn),
        in_specs=[pl.BlockSpec((bm, k), lambda i, j: (i, 0)),
                  pl.BlockSpec((k, bn), lambda i, j: (0, j)),
                  pl.BlockSpec((bm, bn), lambda i, j: (i, j))],
        out_specs=pl.BlockSpec((bm, bn), lambda i, j: (i, j)),
        out_shape=jax.ShapeDtypeStruct((m, n), F32),
        compiler_params=_params(2),
    )(a, w, res)


GLA_ROWS = 512


def _split3(x):
    hi = x.astype(BF16)
    r1 = x - hi.astype(F32)
    mid = r1.astype(BF16)
    lo = (r1 - mid.astype(F32)).astype(BF16)
    return hi, mid, lo


def _gla_kernel(p_ref, a_ref, w2_ref, b_ref, hg_ref, o_ref, state_ref):
    c = GLA_CHUNK

    @pl.when(pl.program_id(1) == 0)
    def _():
        state_ref[...] = jnp.zeros_like(state_ref)

    row = lax.broadcasted_iota(jnp.int32, (c, c), 0)
    col = lax.broadcasted_iota(jnp.int32, (c, c), 1)
    causal = row >= col
    tril = causal.astype(BF16)
    ones = jnp.ones((c, LANES), BF16)
    head_gain = hg_ref[...]
    tn_dims = (((0,), (0,)), ((), ()))
    nt_dims = (((1,), (1,)), ((), ()))

    def chunk(ci, carry):
        rows = pl.ds(pl.multiple_of(ci * c, c), c)
        a_low = a_ref[rows, :].astype(BF16)
        for h in range(GLA_HEADS):
            kq = slice(h * GLA_HK, (h + 1) * GLA_HK)
            q = p_ref[rows, h * GLA_HK:(h + 1) * GLA_HK]
            k = p_ref[rows, GLA_DK + h * GLA_HK:GLA_DK + (h + 1) * GLA_HK]
            v = p_ref[rows, 2 * GLA_DK + h * GLA_HV:2 * GLA_DK + (h + 1) * GLA_HV]
            r = p_ref[rows, 2 * GLA_DK + GLA_DV + h * GLA_HV:2 * GLA_DK + GLA_DV + (h + 1) * GLA_HV]
            z = jnp.dot(a_low, w2_ref[:, kq], preferred_element_type=F32) + b_ref[:, kq]
            log_a = (jnp.minimum(z, 0.0) - jnp.log(1.0 + jnp.exp(-jnp.abs(z)))) / GLA_GATE_TAU
            parts = _split3(log_a)
            bcum = sum(jnp.dot(tril, part, preferred_element_type=F32) for part in parts)
            b_tot_col = sum(lax.dot_general(part, ones, tn_dims, preferred_element_type=F32) for part in parts)
            b_last = bcum[c - 1:c, :]
            q_dec = ((q * (GLA_HK ** -0.5)) * jnp.exp(bcum)).astype(BF16)
            k_inv = (k * jnp.exp(-bcum)).astype(BF16)
            k_end = (k * jnp.exp(b_last - bcum)).astype(BF16)
            v_b = v.astype(BF16)
            attn = jnp.where(causal, lax.dot_general(q_dec, k_inv, nt_dims, preferred_element_type=F32), 0.0)
            state = state_ref[h]
            o = (jnp.dot(attn.astype(BF16), v_b, preferred_element_type=F32)
                 + jnp.dot(q_dec, state.astype(BF16), preferred_element_type=F32))
            kv = lax.dot_general(k_end, v_b, tn_dims, preferred_element_type=F32)
            decay = jnp.exp(b_tot_col)
            state_ref[h] = state * jnp.concatenate([decay] * (GLA_HV // LANES), axis=1) + kv
            ms = jnp.mean(o * o, axis=-1, keepdims=True)
            o = (o * lax.rsqrt(ms + EPS)) * head_gain
            o_ref[rows, h * GLA_HV:(h + 1) * GLA_HV] = (o * _silu(r)).astype(o_ref.dtype)
        return carry

    lax.fori_loop(0, GLA_ROWS // c, chunk, 0)


def _gla_core(proj, a_low, w2, b_alpha, head_gain):
    nblk = SEQ // GLA_ROWS
    return pl.pallas_call(
        _gla_kernel,
        grid=(BATCH, nblk),
        in_specs=[pl.BlockSpec((GLA_ROWS, GLA_MAIN), lambda b, n: (b * nblk + n, 0)),
                  pl.BlockSpec((GLA_ROWS, LANES), lambda b, n: (b * nblk + n, 0)),
                  pl.BlockSpec((LANES, GLA_DK), lambda b, n: (0, 0)),
                  pl.BlockSpec((1, GLA_DK), lambda b, n: (0, 0)),
                  pl.BlockSpec((1, GLA_HV), lambda b, n: (0, 0))],
        out_specs=pl.BlockSpec((GLA_ROWS, GLA_DV), lambda b, n: (b * nblk + n, 0)),
        out_shape=jax.ShapeDtypeStruct((TOKENS, GLA_DV), BF16),
        scratch_shapes=[pltpu.VMEM((GLA_HEADS, GLA_HK, GLA_HV), F32)],
        compiler_params=_params(2),
    )(proj, a_low, w2, b_alpha.reshape(1, GLA_DK), head_gain.reshape(1, GLA_HV))


ATT_TQ = 256
ATT_TK = 256


def _attn_kernel(slope_ref, q_ref, k_ref, v_ref, lq1_ref, lk1_ref, lq2_ref, lk2_ref, hg_ref, o_ref,
                 bias_ref, m_ref, l_ref, acc_ref, *, lambda_init):
    tq, tk, dh = ATT_TQ, ATT_TK, DIFF_HEAD_DIM
    nt_dims = (((1,), (1,)), ((), ()))
    slope = slope_ref[0][:, :1]
    row = lax.broadcasted_iota(jnp.int32, (tq, tk), 0)
    col = lax.broadcasted_iota(jnp.int32, (tq, tk), 1)
    bias_ref[...] = -slope * (row - col).astype(F32)
    lam = (jnp.exp(jnp.sum(lq1_ref[...] * lk1_ref[...], axis=-1, keepdims=True))
           - jnp.exp(jnp.sum(lq2_ref[...] * lk2_ref[...], axis=-1, keepdims=True))
           + lambda_init)
    head_gain = hg_ref[...]

    for qi in range(SEQ // tq):
        q0 = qi * tq
        m_ref[...] = jnp.full(m_ref.shape, MASK_VALUE, F32)
        l_ref[...] = jnp.zeros_like(l_ref)
        acc_ref[...] = jnp.zeros_like(acc_ref)

        def kstep(kj, masked):
            k0 = kj * tk
            krows = pl.ds(pl.multiple_of(k0, tk), tk) if not isinstance(k0, int) else slice(k0, k0 + tk)
            v_blk = v_ref[krows, :]
            shift = -slope * (q0 - k0).astype(F32) if not isinstance(k0, int) else -slope * float(q0 - k0)
            for comp in range(2):
                cs = slice(comp * dh, (comp + 1) * dh)
                s = lax.dot_general(q_ref[q0:q0 + tq, cs], k_ref[krows, cs], nt_dims,
                                    preferred_element_type=F32)
                s = s + bias_ref[...] + shift
                if masked:
                    s = jnp.where(row >= col, s, MASK_VALUE)
                m_prev = m_ref[comp]
                m_new = jnp.maximum(m_prev, jnp.max(s, axis=-1, keepdims=True))
                alpha = jnp.exp(m_prev - m_new)
                p = jnp.exp(s - m_new)
                l_ref[comp] = alpha * l_ref[comp] + jnp.sum(p, axis=-1, keepdims=True)
                acc_ref[comp] = alpha * acc_ref[comp] + jnp.dot(p.astype(BF16), v_blk,
                                                               preferred_element_type=F32)
                m_ref[comp] = m_new

        if qi > 0:
            def body(kj, carry):
                kstep(kj, False)
                return carry
            lax.fori_loop(0, qi, body, 0)
        kstep(qi, True)

        o = acc_ref[0] / l_ref[0] - lam * (acc_ref[1] / l_ref[1])
        ms = jnp.mean(o * o, axis=-1, keepdims=True)
        o = ((o * lax.rsqrt(ms + EPS)) * head_gain) * (1.0 - lambda_init)
        o_ref[q0:q0 + tq, :] = o.astype(o_ref.dtype)


def _diff_attn_core(q, k, v, slopes, lq1, lk1, lq2, lk2, head_gain, lambda_init):
    tq, tk = ATT_TQ, ATT_TK
    assert tq == tk
    blk = pl.BlockSpec((SEQ, DIFF_VDIM), lambda b, h: (b, h))
    vec = pl.BlockSpec((1, DIFF_HEAD_DIM), lambda b, h: (0, 0))
    return pl.pallas_call(
        functools.partial(_attn_kernel, lambda_init=lambda_init),
        grid=(BATCH, DIFF_HEADS),
        in_specs=[pl.BlockSpec((1, 1, LANES), lambda b, h: (h, 0, 0)),
                  blk, blk, blk, vec, vec, vec, vec,
                  pl.BlockSpec((1, DIFF_VDIM), lambda b, h: (0, 0))],
        out_specs=blk,
        out_shape=jax.ShapeDtypeStruct((TOKENS, DIFF_V), BF16),
        scratch_shapes=[pltpu.VMEM((tq, tk), F32),
                        pltpu.VMEM((2, tq, 1), F32),
                        pltpu.VMEM((2, tq, 1), F32),
                        pltpu.VMEM((2, tq, DIFF_VDIM), F32)],
        compiler_params=_params(2),
    )(slopes, q, k, v,
      lq1.reshape(1, DIFF_HEAD_DIM), lk1.reshape(1, DIFF_HEAD_DIM),
      lq2.reshape(1, DIFF_HEAD_DIM), lk2.reshape(1, DIFF_HEAD_DIM),
      head_gain.reshape(1, DIFF_VDIM))


def _ffn(x, gain, w_in, w_out):
    act = _ffn_up(x, gain, w_in, bm=1024, bn=512)
    return _res_mm(act, w_out, x, bm=1024, bn=512)


def kernel(x, gla_attn_norm, gla_w_in, gla_w_alpha2, gla_b_alpha, gla_head_norm, gla_w_out, kv_norm, w_kv, k_norm, diff_attn_norm, diff_w_q, diff_q_norm, diff_lambda_q1, diff_lambda_k1, diff_lambda_q2, diff_lambda_k2, diff_head_norm, diff_w_out, ffn_norm, ffn_w_in, ffn_w_out):
    xt = x.reshape(TOKENS, D_MODEL)
    pad = LANES - GLA_GATE_RANK
    slopes = (2.0 ** (-8.0 * jnp.arange(1, DIFF_HEADS + 1, dtype=F32) / DIFF_HEADS))
    slopes = jnp.broadcast_to(slopes[:, None, None], (DIFF_HEADS, 1, LANES))

    k_shared = v_shared = None
    for l in range(DEPTH):
        if l < N_A_LAYERS:
            i = l
            w_main = gla_w_in[i, :, :GLA_MAIN].astype(BF16)
            w1 = jnp.pad(gla_w_in[i, :, GLA_MAIN:], ((0, 0), (0, pad))).astype(BF16)
            w2 = jnp.pad(gla_w_alpha2[i], ((0, pad), (0, 0))).astype(BF16)
            proj, a_low = _norm_proj(xt, gla_attn_norm[i], w_main, bm=1024, bn=1024, out_dtype=F32, w1=w1)
            o = _gla_core(proj, a_low, w2, gla_b_alpha[i], gla_head_norm[i])
            xt = _res_mm(o, gla_w_out[i].astype(BF16), xt, bm=1024, bn=1024)
        else:
            if l == N_A_LAYERS:
                k_shared = _norm_proj(xt, kv_norm, w_kv[:, :DIFF_QK].astype(BF16), bm=1024, bn=1024,
                                      out_dtype=BF16, gn_gain=k_norm)
                v_shared = _norm_proj(xt, kv_norm, w_kv[:, DIFF_QK:].astype(BF16), bm=1024, bn=1024,
                                      out_dtype=BF16)
            j = l - N_A_LAYERS
            lambda_init = 0.8 - 0.6 * math.exp(-0.3 * l)
            q = _norm_proj(xt, diff_attn_norm[j], diff_w_q[j].astype(BF16), bm=1024, bn=1024,
                           out_dtype=BF16, gn_gain=diff_q_norm[j], scale=DIFF_HEAD_DIM ** -0.5)
            o = _diff_attn_core(q, k_shared, v_shared, slopes, diff_lambda_q1[j], diff_lambda_k1[j],
                                diff_lambda_q2[j], diff_lambda_k2[j], diff_head_norm[j], lambda_init)
            xt = _res_mm(o, diff_w_out[j].astype(BF16), xt, bm=1024, bn=1024)
        xt = _ffn(xt, ffn_norm[l], ffn_w_in[l].astype(BF16), ffn_w_out[l].astype(BF16))
    return xt.reshape(BATCH, SEQ, D_MODEL)
```

```python
import functools
import math

import jax
import jax.numpy as jnp
from jax import lax
from jax.experimental import pallas as pl
from jax.experimental.pallas import tpu as pltpu

D_MODEL = 2048
BATCH = 4
SEQ = 2048
DEPTH = 4
TOKENS = BATCH * SEQ
N_A_LAYERS = DEPTH // 2
N_B_LAYERS = DEPTH - N_A_LAYERS
EPS = 1e-6

GLA_HEADS = 4
GLA_DK = D_MODEL // 2
GLA_DV = D_MODEL
GLA_HK = GLA_DK // GLA_HEADS
GLA_HV = GLA_DV // GLA_HEADS
GLA_GATE_RANK = 16
GLA_GATE_TAU = 16.0
GLA_CHUNK = 64
GLA_MAIN = 2 * GLA_DK + 2 * GLA_DV

DIFF_HEAD_DIM = 128
DIFF_HEADS = D_MODEL // (2 * DIFF_HEAD_DIM)
DIFF_VDIM = 2 * DIFF_HEAD_DIM
DIFF_QK = DIFF_HEADS * 2 * DIFF_HEAD_DIM
DIFF_V = DIFF_HEADS * DIFF_VDIM

D_FF = -(-8 * D_MODEL // (3 * 256)) * 256

LANES = 128
MASK_VALUE = -1e30

BF16 = jnp.bfloat16
F32 = jnp.float32

VMEM_LIMIT = 56 * 1024 * 1024


def _params(n_axes):
    return pltpu.CompilerParams(dimension_semantics=("arbitrary",) * n_axes,
                                vmem_limit_bytes=VMEM_LIMIT)


def _silu(x):
    return x / (1.0 + jnp.exp(-x))


NORM_ROWS = 128


def _fill_normed(x_ref, g_ref, xn_ref):
    gain = g_ref[...]

    def body(i, carry):
        rows = pl.ds(pl.multiple_of(i * NORM_ROWS, NORM_ROWS), NORM_ROWS)
        xv = x_ref[rows, :]
        ms = jnp.mean(xv * xv, axis=-1, keepdims=True)
        xn_ref[rows, :] = ((xv * lax.rsqrt(ms + EPS)) * gain).astype(BF16)
        return carry

    lax.fori_loop(0, x_ref.shape[0] // NORM_ROWS, body, 0)


def _group_norm_store(y, gn_ref, o_ref, scale):
    gain = gn_ref[...]
    for g in range(y.shape[1] // LANES):
        yg = y[:, g * LANES:(g + 1) * LANES]
        ms = jnp.mean(yg * yg, axis=-1, keepdims=True)
        val = (yg * lax.rsqrt(ms + EPS)) * gain
        if scale != 1.0:
            val = val * scale
        o_ref[:, g * LANES:(g + 1) * LANES] = val.astype(o_ref.dtype)


def _norm_proj_kernel(x_ref, g_ref, w_ref, *rest, group_norm, scale, with_gate):
    rest = list(rest)
    w1_ref = rest.pop(0) if with_gate else None
    gn_ref = rest.pop(0) if group_norm else None
    o_ref = rest.pop(0)
    a_ref = rest.pop(0) if with_gate else None
    xn_ref = rest.pop(0)

    @pl.when(pl.program_id(1) == 0)
    def _():
        _fill_normed(x_ref, g_ref, xn_ref)
        if with_gate:
            a_ref[...] = jnp.dot(xn_ref[...], w1_ref[...], preferred_element_type=F32)

    y = jnp.dot(xn_ref[...], w_ref[...], preferred_element_type=F32)
    if group_norm:
        _group_norm_store(y, gn_ref, o_ref, scale)
    else:
        o_ref[...] = y.astype(o_ref.dtype)


def _norm_proj(x, gain, w, *, bm, bn, out_dtype, w1=None, gn_gain=None, scale=1.0):
    m, k = x.shape
    n = w.shape[1]
    with_gate = w1 is not None
    group_norm = gn_gain is not None
    in_specs = [pl.BlockSpec((bm, k), lambda i, j: (i, 0)),
                pl.BlockSpec((1, k), lambda i, j: (0, 0)),
                pl.BlockSpec((k, bn), lambda i, j: (0, j))]
    args = [x, gain.reshape(1, k), w]
    if with_gate:
        in_specs.append(pl.BlockSpec((k, LANES), lambda i, j: (0, 0)))
        args.append(w1)
    if group_norm:
        in_specs.append(pl.BlockSpec((1, LANES), lambda i, j: (0, 0)))
        args.append(gn_gain.reshape(1, LANES))
    out_shape = [jax.ShapeDtypeStruct((m, n), out_dtype)]
    out_specs = [pl.BlockSpec((bm, bn), lambda i, j: (i, j))]
    if with_gate:
        out_shape.append(jax.ShapeDtypeStruct((m, LANES), F32))
        out_specs.append(pl.BlockSpec((bm, LANES), lambda i, j: (i, 0)))
    res = pl.pallas_call(
        functools.partial(_norm_proj_kernel, group_norm=group_norm, scale=scale, with_gate=with_gate),
        grid=(m // bm, n // bn),
        in_specs=in_specs,
        out_specs=out_specs,
        out_shape=out_shape,
        scratch_shapes=[pltpu.VMEM((bm, k), BF16)],
        compiler_params=_params(2),
        name="norm_proj",
    )(*args)
    return res if with_gate else res[0]


def _ffn_up_kernel(x_ref, g_ref, wg_ref, wu_ref, o_ref, xn_ref):
    @pl.when(pl.program_id(1) == 0)
    def _():
        _fill_normed(x_ref, g_ref, xn_ref)

    xn = xn_ref[...]
    g = jnp.dot(xn, wg_ref[...], preferred_element_type=F32)
    u = jnp.dot(xn, wu_ref[...], preferred_element_type=F32)
    o_ref[...] = (_silu(g) * u).astype(o_ref.dtype)


def _ffn_up(x, gain, w_in, *, bm, bn):
    m, k = x.shape
    nblk = D_FF // bn
    return pl.pallas_call(
        _ffn_up_kernel,
        grid=(m // bm, nblk),
        in_specs=[pl.BlockSpec((bm, k), lambda i, j: (i, 0)),
                  pl.BlockSpec((1, k), lambda i, j: (0, 0)),
                  pl.BlockSpec((k, bn), lambda i, j: (0, j)),
                  pl.BlockSpec((k, bn), lambda i, j: (0, j + nblk))],
        out_specs=pl.BlockSpec((bm, bn), lambda i, j: (i, j)),
        out_shape=jax.ShapeDtypeStruct((m, D_FF), BF16),
        scratch_shapes=[pltpu.VMEM((bm, k), BF16)],
        compiler_params=_params(2),
        name="ffn_up",
    )(x, gain.reshape(1, k), w_in, w_in)


def _res_mm_kernel(a_ref, w_ref, r_ref, o_ref):
    o_ref[...] = r_ref[...] + jnp.dot(a_ref[...], w_ref[...], preferred_element_type=F32)


def _res_mm(a, w, res, *, bm, bn):
    m, k = a.shape
    n = w.shape[1]
    return pl.pallas_call(
        _res_mm_kernel,
        grid=(m // bm, n // bn),
        in_specs=[pl.BlockSpec((bm, k), lambda i, j: (i, 0)),
                  pl.BlockSpec((k, bn), lambda i, j: (0, j)),
                  pl.BlockSpec((bm, bn), lambda i, j: (i, j))],
        out_specs=pl.BlockSpec((bm, bn), lambda i, j: (i, j)),
        out_shape=jax.ShapeDtypeStruct((m, n), F32),
        compiler_params=_params(2),
        name="res_mm",
    )(a, w, res)


GLA_ROWS = 512


def _split3(x):
    hi = x.astype(BF16)
    r1 = x - hi.astype(F32)
    mid = r1.astype(BF16)
    lo = (r1 - mid.astype(F32)).astype(BF16)
    return hi, mid, lo


def _gla_kernel(p_ref, a_ref, w2_ref, b_ref, hg_ref, o_ref, state_ref):
    c = GLA_CHUNK

    @pl.when(pl.program_id(1) == 0)
    def _():
        state_ref[...] = jnp.zeros_like(state_ref)

    row = lax.broadcasted_iota(jnp.int32, (c, c), 0)
    col = lax.broadcasted_iota(jnp.int32, (c, c), 1)
    causal = row >= col
    tril = causal.astype(BF16)
    ones = jnp.ones((c, LANES), BF16)
    head_gain = hg_ref[...]
    tn_dims = (((0,), (0,)), ((), ()))
    nt_dims = (((1,), (1,)), ((), ()))

    def chunk(ci, carry):
        rows = pl.ds(pl.multiple_of(ci * c, c), c)
        a_low = a_ref[rows, :].astype(BF16)
        for h in range(GLA_HEADS):
            kq = slice(h * GLA_HK, (h + 1) * GLA_HK)
            q = p_ref[rows, h * GLA_HK:(h + 1) * GLA_HK]
            k = p_ref[rows, GLA_DK + h * GLA_HK:GLA_DK + (h + 1) * GLA_HK]
            v = p_ref[rows, 2 * GLA_DK + h * GLA_HV:2 * GLA_DK + (h + 1) * GLA_HV]
            r = p_ref[rows, 2 * GLA_DK + GLA_DV + h * GLA_HV:2 * GLA_DK + GLA_DV + (h + 1) * GLA_HV]
            z = jnp.dot(a_low, w2_ref[:, kq], preferred_element_type=F32) + b_ref[:, kq]
            log_a = (jnp.minimum(z, 0.0) - jnp.log(1.0 + jnp.exp(-jnp.abs(z)))) / GLA_GATE_TAU
            parts = _split3(log_a)
            bcum = sum(jnp.dot(tril, part, preferred_element_type=F32) for part in parts)
            b_tot_col = sum(lax.dot_general(part, ones, tn_dims, preferred_element_type=F32) for part in parts)
            b_last = bcum[c - 1:c, :]
            q_dec = ((q * (GLA_HK ** -0.5)) * jnp.exp(bcum)).astype(BF16)
            k_inv = (k * jnp.exp(-bcum)).astype(BF16)
            k_end = (k * jnp.exp(b_last - bcum)).astype(BF16)
            v_b = v.astype(BF16)
            attn = jnp.where(causal, lax.dot_general(q_dec, k_inv, nt_dims, preferred_element_type=F32), 0.0)
            state = state_ref[h]
            o = (jnp.dot(attn.astype(BF16), v_b, preferred_element_type=F32)
                 + jnp.dot(q_dec, state.astype(BF16), preferred_element_type=F32))
            kv = lax.dot_general(k_end, v_b, tn_dims, preferred_element_type=F32)
            decay = jnp.exp(b_tot_col)
            state_ref[h] = state * jnp.concatenate([decay] * (GLA_HV // LANES), axis=1) + kv
            ms = jnp.mean(o * o, axis=-1, keepdims=True)
            o = (o * lax.rsqrt(ms + EPS)) * head_gain
            o_ref[rows, h * GLA_HV:(h + 1) * GLA_HV] = (o * _silu(r)).astype(o_ref.dtype)
        return carry

    lax.fori_loop(0, GLA_ROWS // c, chunk, 0)


def _gla_core(proj, a_low, w2, b_alpha, head_gain):
    nblk = SEQ // GLA_ROWS
    return pl.pallas_call(
        _gla_kernel,
        grid=(BATCH, nblk),
        in_specs=[pl.BlockSpec((GLA_ROWS, GLA_MAIN), lambda b, n: (b * nblk + n, 0)),
                  pl.BlockSpec((GLA_ROWS, LANES), lambda b, n: (b * nblk + n, 0)),
                  pl.BlockSpec((LANES, GLA_DK), lambda b, n: (0, 0)),
                  pl.BlockSpec((1, GLA_DK), lambda b, n: (0, 0)),
                  pl.BlockSpec((1, GLA_HV), lambda b, n: (0, 0))],
        out_specs=pl.BlockSpec((GLA_ROWS, GLA_DV), lambda b, n: (b * nblk + n, 0)),
        out_shape=jax.ShapeDtypeStruct((TOKENS, GLA_DV), BF16),
        scratch_shapes=[pltpu.VMEM((GLA_HEADS, GLA_HK, GLA_HV), F32)],
        compiler_params=_params(2),
        name="gla_core",
    )(proj, a_low, w2, b_alpha.reshape(1, GLA_DK), head_gain.reshape(1, GLA_HV))


ATT_TQ = 512
ATT_TK = 512
LOG2E = math.log2(math.e)


def _lane_tiles(x):
    return [x[:, i * LANES:(i + 1) * LANES] for i in range(x.shape[1] // LANES)]


def _attn_kernel(slope_ref, q_ref, k_ref, v_ref, lq1_ref, lk1_ref, lq2_ref, lk2_ref, hg_ref, o_ref,
                 bias_ref, biasd_ref, m_ref, l_ref, acc_ref, *, lambda_init):
    tq, tk, dh = ATT_TQ, ATT_TK, DIFF_HEAD_DIM
    nt_dims = (((1,), (1,)), ((), ()))
    slope2 = slope_ref[0][:, :1] * LOG2E
    row = lax.broadcasted_iota(jnp.int32, (tq, tk), 0)
    col = lax.broadcasted_iota(jnp.int32, (tq, tk), 1)
    bias = -slope2 * (row - col).astype(F32)
    bias_ref[...] = bias
    biasd_ref[...] = jnp.where(row >= col, bias, MASK_VALUE)
    lam = (jnp.exp(jnp.sum(lq1_ref[...] * lk1_ref[...], axis=-1, keepdims=True))
           - jnp.exp(jnp.sum(lq2_ref[...] * lk2_ref[...], axis=-1, keepdims=True))
           + lambda_init)
    head_gain = hg_ref[...]

    def kstep(q0, k0, krows, b_ref):
        if isinstance(k0, int):
            shift = -slope2 * float(q0 - k0)
        else:
            shift = -slope2 * (q0 - k0).astype(F32)
        v_blk = v_ref[krows, :]
        for comp in range(2):
            cs = slice(comp * dh, (comp + 1) * dh)
            t = lax.dot_general(q_ref[q0:q0 + tq, cs], k_ref[krows, cs], nt_dims,
                                preferred_element_type=F32) + b_ref[...]
            tiles = _lane_tiles(t)
            tmax = functools.reduce(jnp.maximum, tiles)
            m_prev = m_ref[comp]
            m_new = jnp.maximum(m_prev, jnp.max(tmax, axis=-1, keepdims=True) + shift)
            alpha = jnp.exp2(m_prev - m_new)
            off = shift - m_new
            p_tiles = [jnp.exp2(ti + off) for ti in tiles]
            l_ref[comp] = alpha * l_ref[comp] + functools.reduce(jnp.add, p_tiles)
            p = jnp.concatenate([pt.astype(BF16) for pt in p_tiles], axis=1)
            pv = jnp.dot(p, v_blk, preferred_element_type=F32)
            acc_ref[comp] = jnp.concatenate([alpha] * (DIFF_VDIM // LANES), axis=1) * acc_ref[comp] + pv
            m_ref[comp] = m_new

    for qi in range(SEQ // tq):
        q0 = qi * tq
        m_ref[...] = jnp.full(m_ref.shape, MASK_VALUE, F32)
        l_ref[...] = jnp.zeros_like(l_ref)
        acc_ref[...] = jnp.zeros_like(acc_ref)

        for kj in range(qi):
            kstep(q0, kj * tk, slice(kj * tk, (kj + 1) * tk), bias_ref)
        kstep(q0, q0, slice(q0, q0 + tk), biasd_ref)

        l0 = jnp.sum(l_ref[0], axis=-1, keepdims=True)
        l1 = jnp.sum(l_ref[1], axis=-1, keepdims=True)
        o = acc_ref[0] / l0 - lam * (acc_ref[1] / l1)
        ms = jnp.mean(o * o, axis=-1, keepdims=True)
        o = ((o * lax.rsqrt(ms + EPS)) * head_gain) * (1.0 - lambda_init)
        o_ref[q0:q0 + tq, :] = o.astype(o_ref.dtype)


def _diff_attn_core(q, k, v, slopes, lq1, lk1, lq2, lk2, head_gain, lambda_init):
    tq, tk = ATT_TQ, ATT_TK
    assert tq == tk
    blk = pl.BlockSpec((SEQ, DIFF_VDIM), lambda b, h: (b, h))
    vec = pl.BlockSpec((1, DIFF_HEAD_DIM), lambda b, h: (0, 0))
    return pl.pallas_call(
        functools.partial(_attn_kernel, lambda_init=lambda_init),
        grid=(BATCH, DIFF_HEADS),
        in_specs=[pl.BlockSpec((1, 1, LANES), lambda b, h: (h, 0, 0)),
                  blk, blk, blk, vec, vec, vec, vec,
                  pl.BlockSpec((1, DIFF_VDIM), lambda b, h: (0, 0))],
        out_specs=blk,
        out_shape=jax.ShapeDtypeStruct((TOKENS, DIFF_V), BF16),
        scratch_shapes=[pltpu.VMEM((tq, tk), F32),
                        pltpu.VMEM((tq, tk), F32),
                        pltpu.VMEM((2, tq, LANES), F32),
                        pltpu.VMEM((2, tq, LANES), F32),
                        pltpu.VMEM((2, tq, DIFF_VDIM), F32)],
        compiler_params=_params(2),
        name="diff_attn",
    )(slopes, q, k, v,
      lq1.reshape(1, DIFF_HEAD_DIM), lk1.reshape(1, DIFF_HEAD_DIM),
      lq2.reshape(1, DIFF_HEAD_DIM), lk2.reshape(1, DIFF_HEAD_DIM),
      head_gain.reshape(1, DIFF_VDIM))


def _ffn(x, gain, w_in, w_out):
    act = _ffn_up(x, gain, w_in, bm=1024, bn=512)
    return _res_mm(act, w_out, x, bm=1024, bn=512)


def kernel(x, gla_attn_norm, gla_w_in, gla_w_alpha2, gla_b_alpha, gla_head_norm, gla_w_out, kv_norm, w_kv, k_norm, diff_attn_norm, diff_w_q, diff_q_norm, diff_lambda_q1, diff_lambda_k1, diff_lambda_q2, diff_lambda_k2, diff_head_norm, diff_w_out, ffn_norm, ffn_w_in, ffn_w_out):
    xt = x.reshape(TOKENS, D_MODEL)
    pad = LANES - GLA_GATE_RANK
    slopes = (2.0 ** (-8.0 * jnp.arange(1, DIFF_HEADS + 1, dtype=F32) / DIFF_HEADS))
    slopes = jnp.broadcast_to(slopes[:, None, None], (DIFF_HEADS, 1, LANES))

    k_shared = v_shared = None
    for l in range(DEPTH):
        if l < N_A_LAYERS:
            i = l
            w_main = gla_w_in[i, :, :GLA_MAIN].astype(BF16)
            w1 = jnp.pad(gla_w_in[i, :, GLA_MAIN:], ((0, 0), (0, pad))).astype(BF16)
            w2 = jnp.pad(gla_w_alpha2[i], ((0, pad), (0, 0))).astype(BF16)
            proj, a_low = _norm_proj(xt, gla_attn_norm[i], w_main, bm=1024, bn=1024, out_dtype=F32, w1=w1)
            o = _gla_core(proj, a_low, w2, gla_b_alpha[i], gla_head_norm[i])
            xt = _res_mm(o, gla_w_out[i].astype(BF16), xt, bm=1024, bn=1024)
        else:
            if l == N_A_LAYERS:
                k_shared = _norm_proj(xt, kv_norm, w_kv[:, :DIFF_QK].astype(BF16), bm=1024, bn=1024,
                                      out_dtype=BF16, gn_gain=k_norm)
                v_shared = _norm_proj(xt, kv_norm, w_kv[:, DIFF_QK:].astype(BF16), bm=1024, bn=1024,
                                      out_dtype=BF16)
            j = l - N_A_LAYERS
            lambda_init = 0.8 - 0.6 * math.exp(-0.3 * l)
            q = _norm_proj(xt, diff_attn_norm[j], diff_w_q[j].astype(BF16), bm=1024, bn=1024,
                           out_dtype=BF16, gn_gain=diff_q_norm[j], scale=DIFF_HEAD_DIM ** -0.5 * LOG2E)
            o = _diff_attn_core(q, k_shared, v_shared, slopes, diff_lambda_q1[j], diff_lambda_k1[j],
                                diff_lambda_q2[j], diff_lambda_k2[j], diff_head_norm[j], lambda_init)
            xt = _res_mm(o, diff_w_out[j].astype(BF16), xt, bm=1024, bn=1024)
        xt = _ffn(xt, ffn_norm[l], ffn_w_in[l].astype(BF16), ffn_w_out[l].astype(BF16))
    return xt.reshape(BATCH, SEQ, D_MODEL)
```

```python
import functools
import math

import jax
import jax.numpy as jnp
from jax import lax
from jax.experimental import pallas as pl
from jax.experimental.pallas import tpu as pltpu

D_MODEL = 2048
BATCH = 4
SEQ = 2048
DEPTH = 4
TOKENS = BATCH * SEQ
N_A_LAYERS = DEPTH // 2
N_B_LAYERS = DEPTH - N_A_LAYERS
EPS = 1e-6

GLA_HEADS = 4
GLA_DK = D_MODEL // 2
GLA_DV = D_MODEL
GLA_HK = GLA_DK // GLA_HEADS
GLA_HV = GLA_DV // GLA_HEADS
GLA_GATE_RANK = 16
GLA_GATE_TAU = 16.0
GLA_CHUNK = 64
GLA_MAIN = 2 * GLA_DK + 2 * GLA_DV

DIFF_HEAD_DIM = 128
DIFF_HEADS = D_MODEL // (2 * DIFF_HEAD_DIM)
DIFF_VDIM = 2 * DIFF_HEAD_DIM
DIFF_QK = DIFF_HEADS * 2 * DIFF_HEAD_DIM
DIFF_V = DIFF_HEADS * DIFF_VDIM

D_FF = -(-8 * D_MODEL // (3 * 256)) * 256

LANES = 128
MASK_VALUE = -1e30

BF16 = jnp.bfloat16
F32 = jnp.float32

VMEM_LIMIT = 56 * 1024 * 1024


def _params(n_axes):
    return pltpu.CompilerParams(dimension_semantics=("arbitrary",) * n_axes,
                                vmem_limit_bytes=VMEM_LIMIT)


def _silu(x):
    return x / (1.0 + jnp.exp(-x))


NORM_ROWS = 128


def _fill_normed(x_ref, g_ref, xn_ref):
    gain = g_ref[...]

    def body(i, carry):
        rows = pl.ds(pl.multiple_of(i * NORM_ROWS, NORM_ROWS), NORM_ROWS)
        xv = x_ref[rows, :]
        ms = jnp.mean(xv * xv, axis=-1, keepdims=True)
        xn_ref[rows, :] = ((xv * lax.rsqrt(ms + EPS)) * gain).astype(BF16)
        return carry

    lax.fori_loop(0, x_ref.shape[0] // NORM_ROWS, body, 0)


def _group_norm_store(y, gn_ref, o_ref, scale):
    gain = gn_ref[...]
    for g in range(y.shape[1] // LANES):
        yg = y[:, g * LANES:(g + 1) * LANES]
        ms = jnp.mean(yg * yg, axis=-1, keepdims=True)
        val = (yg * lax.rsqrt(ms + EPS)) * gain
        if scale != 1.0:
            val = val * scale
        o_ref[:, g * LANES:(g + 1) * LANES] = val.astype(o_ref.dtype)


def _norm_proj_kernel(x_ref, g_ref, w_ref, *rest, group_norm, scale, with_gate):
    rest = list(rest)
    w1_ref = rest.pop(0) if with_gate else None
    gn_ref = rest.pop(0) if group_norm else None
    o_ref = rest.pop(0)
    a_ref = rest.pop(0) if with_gate else None
    xn_ref = rest.pop(0)

    @pl.when(pl.program_id(1) == 0)
    def _():
        _fill_normed(x_ref, g_ref, xn_ref)
        if with_gate:
            a_ref[...] = jnp.dot(xn_ref[...], w1_ref[...], preferred_element_type=F32)

    y = jnp.dot(xn_ref[...], w_ref[...], preferred_element_type=F32)
    if group_norm:
        _group_norm_store(y, gn_ref, o_ref, scale)
    else:
        o_ref[...] = y.astype(o_ref.dtype)


def _w_spec(k, bn, layer, col_block0):
    return pl.BlockSpec((None, k, bn), lambda i, j: (layer, 0, j + col_block0))


def _norm_proj(x, gain, w, layer, *, n, bm, bn, out_dtype, col0=0, w1=None, gn_gain=None, scale=1.0):
    m, k = x.shape
    with_gate = w1 is not None
    group_norm = gn_gain is not None
    in_specs = [pl.BlockSpec((bm, k), lambda i, j: (i, 0)),
                pl.BlockSpec((1, k), lambda i, j: (0, 0)),
                _w_spec(k, bn, layer, col0 // bn)]
    args = [x, gain.reshape(1, k), w]
    if with_gate:
        in_specs.append(pl.BlockSpec((k, LANES), lambda i, j: (0, 0)))
        args.append(w1)
    if group_norm:
        in_specs.append(pl.BlockSpec((1, LANES), lambda i, j: (0, 0)))
        args.append(gn_gain.reshape(1, LANES))
    out_shape = [jax.ShapeDtypeStruct((m, n), out_dtype)]
    out_specs = [pl.BlockSpec((bm, bn), lambda i, j: (i, j))]
    if with_gate:
        out_shape.append(jax.ShapeDtypeStruct((m, LANES), F32))
        out_specs.append(pl.BlockSpec((bm, LANES), lambda i, j: (i, 0)))
    res = pl.pallas_call(
        functools.partial(_norm_proj_kernel, group_norm=group_norm, scale=scale, with_gate=with_gate),
        grid=(m // bm, n // bn),
        in_specs=in_specs,
        out_specs=out_specs,
        out_shape=out_shape,
        scratch_shapes=[pltpu.VMEM((bm, k), BF16)],
        compiler_params=_params(2),
        name="norm_proj",
    )(*args)
    return res if with_gate else res[0]


def _ffn_up_kernel(x_ref, g_ref, wg_ref, wu_ref, o_ref, xn_ref):
    @pl.when(pl.program_id(1) == 0)
    def _():
        _fill_normed(x_ref, g_ref, xn_ref)

    xn = xn_ref[...]
    g = jnp.dot(xn, wg_ref[...], preferred_element_type=F32)
    u = jnp.dot(xn, wu_ref[...], preferred_element_type=F32)
    o_ref[...] = (_silu(g) * u).astype(o_ref.dtype)


def _ffn_up(x, gain, w_in, layer, *, bm, bn):
    m, k = x.shape
    nblk = D_FF // bn
    return pl.pallas_call(
        _ffn_up_kernel,
        grid=(m // bm, nblk),
        in_specs=[pl.BlockSpec((bm, k), lambda i, j: (i, 0)),
                  pl.BlockSpec((1, k), lambda i, j: (0, 0)),
                  _w_spec(k, bn, layer, 0),
                  _w_spec(k, bn, layer, nblk)],
        out_specs=pl.BlockSpec((bm, bn), lambda i, j: (i, j)),
        out_shape=jax.ShapeDtypeStruct((m, D_FF), BF16),
        scratch_shapes=[pltpu.VMEM((bm, k), BF16)],
        compiler_params=_params(2),
        name="ffn_up",
    )(x, gain.reshape(1, k), w_in, w_in)


def _res_mm_kernel(a_ref, w_ref, r_ref, o_ref):
    o_ref[...] = r_ref[...] + jnp.dot(a_ref[...], w_ref[...], preferred_element_type=F32)


def _res_mm(a, w, layer, res, *, bm, bn):
    m, k = a.shape
    n = w.shape[2]
    return pl.pallas_call(
        _res_mm_kernel,
        grid=(m // bm, n // bn),
        in_specs=[pl.BlockSpec((bm, k), lambda i, j: (i, 0)),
                  _w_spec(k, bn, layer, 0),
                  pl.BlockSpec((bm, bn), lambda i, j: (i, j))],
        out_specs=pl.BlockSpec((bm, bn), lambda i, j: (i, j)),
        out_shape=jax.ShapeDtypeStruct((m, n), F32),
        compiler_params=_params(2),
        name="res_mm",
    )(a, w, res)


GLA_ROWS = 512


def _split2(x):
    hi = x.astype(BF16)
    lo = (x - hi.astype(F32)).astype(BF16)
    return hi, lo


def _gla_kernel(p_ref, a_ref, w2_ref, b_ref, hg_ref, o_ref, state_ref):
    c = GLA_CHUNK

    @pl.when(pl.program_id(1) == 0)
    def _():
        state_ref[...] = jnp.zeros_like(state_ref)

    row = lax.broadcasted_iota(jnp.int32, (c, c), 0)
    col = lax.broadcasted_iota(jnp.int32, (c, c), 1)
    causal = row >= col
    tril = causal.astype(BF16)
    head_gain = hg_ref[...]
    tn_dims = (((0,), (0,)), ((), ()))
    nt_dims = (((1,), (1,)), ((), ()))

    def chunk(ci, carry):
        rows = pl.ds(pl.multiple_of(ci * c, c), c)
        heads = range(GLA_HEADS)
        a_low = a_ref[rows, :].astype(BF16)
        z = jnp.dot(a_low, w2_ref[...], preferred_element_type=F32) + b_ref[...]
        log_a = (jnp.minimum(z, 0.0) - jnp.log(1.0 + jnp.exp(-jnp.abs(z)))) / GLA_GATE_TAU
        bcum = sum(jnp.dot(tril, part, preferred_element_type=F32) for part in _split2(log_a))
        b_last = bcum[c - 1:c, :]
        q = p_ref[rows, 0:GLA_DK]
        k = p_ref[rows, GLA_DK:2 * GLA_DK]
        q_dec = ((q * (GLA_HK ** -0.5)) * jnp.exp(bcum)).astype(BF16)
        k_inv = (k * jnp.exp(-bcum)).astype(BF16)
        k_end = (k * jnp.exp(b_last - bcum)).astype(BF16)
        chunk_decay = jnp.exp(b_last)
        v_b = p_ref[rows, 2 * GLA_DK:2 * GLA_DK + GLA_DV].astype(BF16)
        hk = [slice(h * GLA_HK, (h + 1) * GLA_HK) for h in heads]
        hv = [slice(h * GLA_HV, (h + 1) * GLA_HV) for h in heads]
        states = [state_ref[h] for h in heads]
        o_inter = [jnp.dot(q_dec[:, hk[h]], states[h].astype(BF16), preferred_element_type=F32) for h in heads]
        kv = [lax.dot_general(k_end[:, hk[h]], v_b[:, hv[h]], tn_dims, preferred_element_type=F32)
              for h in heads]
        attn = [lax.dot_general(q_dec[:, hk[h]], k_inv[:, hk[h]], nt_dims, preferred_element_type=F32)
                for h in heads]
        o_intra = [jnp.dot(jnp.where(causal, attn[h], 0.0).astype(BF16), v_b[:, hv[h]],
                           preferred_element_type=F32) for h in heads]
        for h in heads:
            decay = jnp.broadcast_to(chunk_decay[:, hk[h]], (LANES, GLA_HK)).T
            state_ref[h] = states[h] * jnp.concatenate([decay] * (GLA_HV // LANES), axis=1) + kv[h]
            o = o_intra[h] + o_inter[h]
            ms = jnp.mean(o * o, axis=-1, keepdims=True)
            o = (o * lax.rsqrt(ms + EPS)) * head_gain
            r = p_ref[rows, 2 * GLA_DK + GLA_DV + h * GLA_HV:2 * GLA_DK + GLA_DV + (h + 1) * GLA_HV]
            o_ref[rows, hv[h]] = (o * _silu(r)).astype(o_ref.dtype)
        return carry

    lax.fori_loop(0, GLA_ROWS // c, chunk, 0, unroll=2)


def _gla_core(proj, a_low, w2, b_alpha, head_gain):
    nblk = SEQ // GLA_ROWS
    return pl.pallas_call(
        _gla_kernel,
        grid=(BATCH, nblk),
        in_specs=[pl.BlockSpec((GLA_ROWS, GLA_MAIN), lambda b, n: (b * nblk + n, 0)),
                  pl.BlockSpec((GLA_ROWS, LANES), lambda b, n: (b * nblk + n, 0)),
                  pl.BlockSpec((LANES, GLA_DK), lambda b, n: (0, 0)),
                  pl.BlockSpec((1, GLA_DK), lambda b, n: (0, 0)),
                  pl.BlockSpec((1, GLA_HV), lambda b, n: (0, 0))],
        out_specs=pl.BlockSpec((GLA_ROWS, GLA_DV), lambda b, n: (b * nblk + n, 0)),
        out_shape=jax.ShapeDtypeStruct((TOKENS, GLA_DV), BF16),
        scratch_shapes=[pltpu.VMEM((GLA_HEADS, GLA_HK, GLA_HV), F32)],
        compiler_params=_params(2),
        name="gla_core",
    )(proj, a_low, w2, b_alpha.reshape(1, GLA_DK), head_gain.reshape(1, GLA_HV))


ATT_TQ = 512
ATT_TK = 512
LOG2E = math.log2(math.e)


def _lane_tiles(x):
    return [x[:, i * LANES:(i + 1) * LANES] for i in range(x.shape[1] // LANES)]


def _attn_kernel(slope_ref, q_ref, k_ref, v_ref, lq1_ref, lk1_ref, lq2_ref, lk2_ref, hg_ref, o_ref,
                 bias_ref, biasd_ref, m_ref, l_ref, acc_ref, *, lambda_init):
    tq, tk, dh = ATT_TQ, ATT_TK, DIFF_HEAD_DIM
    nt_dims = (((1,), (1,)), ((), ()))
    slope2 = slope_ref[0][:, :1] * LOG2E
    row = lax.broadcasted_iota(jnp.int32, (tq, tk), 0)
    col = lax.broadcasted_iota(jnp.int32, (tq, tk), 1)
    bias = -slope2 * (row - col).astype(F32)
    bias_ref[...] = bias
    biasd_ref[...] = jnp.where(row >= col, bias, MASK_VALUE)
    lam = (jnp.exp(jnp.sum(lq1_ref[...] * lk1_ref[...], axis=-1, keepdims=True))
           - jnp.exp(jnp.sum(lq2_ref[...] * lk2_ref[...], axis=-1, keepdims=True))
           + lambda_init)
    head_gain = hg_ref[...]

    def kstep(q0, k0, krows, b_ref):
        if isinstance(k0, int):
            shift = -slope2 * float(q0 - k0)
        else:
            shift = -slope2 * (q0 - k0).astype(F32)
        v_blk = v_ref[krows, :]
        for comp in range(2):
            cs = slice(comp * dh, (comp + 1) * dh)
            t = lax.dot_general(q_ref[q0:q0 + tq, cs], k_ref[krows, cs], nt_dims,
                                preferred_element_type=F32) + b_ref[...]
            tiles = _lane_tiles(t)
            tmax = functools.reduce(jnp.maximum, tiles)
            m_prev = m_ref[comp]
            m_new = jnp.maximum(m_prev, jnp.max(tmax, axis=-1, keepdims=True) + shift)
            alpha = jnp.exp2(m_prev - m_new)
            off = shift - m_new
            p_tiles = [jnp.exp2(ti + off) for ti in tiles]
            l_ref[comp] = alpha * l_ref[comp] + functools.reduce(jnp.add, p_tiles)
            p = jnp.concatenate([pt.astype(BF16) for pt in p_tiles], axis=1)
            pv = jnp.dot(p, v_blk, preferred_element_type=F32)
            acc_ref[comp] = jnp.concatenate([alpha] * (DIFF_VDIM // LANES), axis=1) * acc_ref[comp] + pv
            m_ref[comp] = m_new

    for qi in range(SEQ // tq):
        q0 = qi * tq
        m_ref[...] = jnp.full(m_ref.shape, MASK_VALUE, F32)
        l_ref[...] = jnp.zeros_like(l_ref)
        acc_ref[...] = jnp.zeros_like(acc_ref)

        for kj in range(qi):
            kstep(q0, kj * tk, slice(kj * tk, (kj + 1) * tk), bias_ref)
        kstep(q0, q0, slice(q0, q0 + tk), biasd_ref)

        l0 = jnp.sum(l_ref[0], axis=-1, keepdims=True)
        l1 = jnp.sum(l_ref[1], axis=-1, keepdims=True)
        o = acc_ref[0] / l0 - lam * (acc_ref[1] / l1)
        ms = jnp.mean(o * o, axis=-1, keepdims=True)
        o = ((o * lax.rsqrt(ms + EPS)) * head_gain) * (1.0 - lambda_init)
        o_ref[q0:q0 + tq, :] = o.astype(o_ref.dtype)


def _diff_attn_core(q, k, v, slopes, lq1, lk1, lq2, lk2, head_gain, lambda_init):
    tq, tk = ATT_TQ, ATT_TK
    assert tq == tk
    blk = pl.BlockSpec((SEQ, DIFF_VDIM), lambda b, h: (b, h))
    vec = pl.BlockSpec((1, DIFF_HEAD_DIM), lambda b, h: (0, 0))
    return pl.pallas_call(
        functools.partial(_attn_kernel, lambda_init=lambda_init),
        grid=(BATCH, DIFF_HEADS),
        in_specs=[pl.BlockSpec((1, 1, LANES), lambda b, h: (h, 0, 0)),
                  blk, blk, blk, vec, vec, vec, vec,
                  pl.BlockSpec((1, DIFF_VDIM), lambda b, h: (0, 0))],
        out_specs=blk,
        out_shape=jax.ShapeDtypeStruct((TOKENS, DIFF_V), BF16),
        scratch_shapes=[pltpu.VMEM((tq, tk), F32),
                        pltpu.VMEM((tq, tk), F32),
                        pltpu.VMEM((2, tq, LANES), F32),
                        pltpu.VMEM((2, tq, LANES), F32),
                        pltpu.VMEM((2, tq, DIFF_VDIM), F32)],
        compiler_params=_params(2),
        name="diff_attn",
    )(slopes, q, k, v,
      lq1.reshape(1, DIFF_HEAD_DIM), lk1.reshape(1, DIFF_HEAD_DIM),
      lq2.reshape(1, DIFF_HEAD_DIM), lk2.reshape(1, DIFF_HEAD_DIM),
      head_gain.reshape(1, DIFF_VDIM))


CAST_BLOCK_BYTES = 8 * 1024 * 1024


def _cast_kernel(x_ref, o_ref):
    o_ref[...] = x_ref[...].astype(o_ref.dtype)


def _to_bf16(w):
    l, r, c = w.shape
    rows = l * r
    bc = next((t for t in (2048, 1024) if c % t == 0), c)
    br = next(t for t in (2048, 1024, 512, 256, 128) if rows % t == 0 and t * bc * 4 <= CAST_BLOCK_BYTES)
    out = pl.pallas_call(
        _cast_kernel,
        grid=(rows // br, c // bc),
        in_specs=[pl.BlockSpec((br, bc), lambda i, j: (i, j))],
        out_specs=pl.BlockSpec((br, bc), lambda i, j: (i, j)),
        out_shape=jax.ShapeDtypeStruct((rows, c), BF16),
        compiler_params=_params(2),
        name="cast_bf16",
    )(w.reshape(rows, c))
    return out.reshape(l, r, c)


def _ffn(x, gain, w_in, w_out, layer):
    act = _ffn_up(x, gain, w_in, layer, bm=1024, bn=512)
    return _res_mm(act, w_out, layer, x, bm=1024, bn=512)


def kernel(x, gla_attn_norm, gla_w_in, gla_w_alpha2, gla_b_alpha, gla_head_norm, gla_w_out, kv_norm, w_kv, k_norm, diff_attn_norm, diff_w_q, diff_q_norm, diff_lambda_q1, diff_lambda_k1, diff_lambda_q2, diff_lambda_k2, diff_head_norm, diff_w_out, ffn_norm, ffn_w_in, ffn_w_out):
    xt = x.reshape(TOKENS, D_MODEL)
    pad = LANES - GLA_GATE_RANK
    slopes = (2.0 ** (-8.0 * jnp.arange(1, DIFF_HEADS + 1, dtype=F32) / DIFF_HEADS))
    slopes = jnp.broadcast_to(slopes[:, None, None], (DIFF_HEADS, 1, LANES))

    gla_in_b = _to_bf16(gla_w_in)
    gla_out_b = _to_bf16(gla_w_out)
    kv_b = _to_bf16(w_kv[None])
    wq_b = _to_bf16(diff_w_q)
    wo_b = _to_bf16(diff_w_out)
    ffn_in_b = _to_bf16(ffn_w_in)
    ffn_out_b = _to_bf16(ffn_w_out)

    k_shared = v_shared = None
    for l in range(DEPTH):
        if l < N_A_LAYERS:
            i = l
            w1 = jnp.pad(gla_in_b[i, :, GLA_MAIN:], ((0, 0), (0, pad)))
            w2 = jnp.pad(gla_w_alpha2[i], ((0, pad), (0, 0))).astype(BF16)
            proj, a_low = _norm_proj(xt, gla_attn_norm[i], gla_in_b, i, n=GLA_MAIN, bm=1024, bn=1024,
                                     out_dtype=F32, w1=w1)
            o = _gla_core(proj, a_low, w2, gla_b_alpha[i], gla_head_norm[i])
            xt = _res_mm(o, gla_out_b, i, xt, bm=1024, bn=1024)
        else:
            if l == N_A_LAYERS:
                k_shared = _norm_proj(xt, kv_norm, kv_b, 0, n=DIFF_QK, bm=1024, bn=1024,
                                      out_dtype=BF16, gn_gain=k_norm)
                v_shared = _norm_proj(xt, kv_norm, kv_b, 0, n=DIFF_V, col0=DIFF_QK, bm=1024, bn=1024,
                                      out_dtype=BF16)
            j = l - N_A_LAYERS
            lambda_init = 0.8 - 0.6 * math.exp(-0.3 * l)
            q = _norm_proj(xt, diff_attn_norm[j], wq_b, j, n=DIFF_QK, bm=1024, bn=1024,
                           out_dtype=BF16, gn_gain=diff_q_norm[j], scale=DIFF_HEAD_DIM ** -0.5 * LOG2E)
            o = _diff_attn_core(q, k_shared, v_shared, slopes, diff_lambda_q1[j], diff_lambda_k1[j],
                                diff_lambda_q2[j], diff_lambda_k2[j], diff_head_norm[j], lambda_init)
            xt = _res_mm(o, wo_b, j, xt, bm=1024, bn=1024)
        xt = _ffn(xt, ffn_norm[l], ffn_in_b, ffn_out_b, l)
    return xt.reshape(BATCH, SEQ, D_MODEL)
```

```python
import functools
import math

import jax
import jax.numpy as jnp
from jax import lax
from jax.experimental import pallas as pl
from jax.experimental.pallas import tpu as pltpu

D_MODEL = 2048
BATCH = 4
SEQ = 2048
DEPTH = 4
TOKENS = BATCH * SEQ
N_A_LAYERS = DEPTH // 2
N_B_LAYERS = DEPTH - N_A_LAYERS
EPS = 1e-6

GLA_HEADS = 4
GLA_DK = D_MODEL // 2
GLA_DV = D_MODEL
GLA_HK = GLA_DK // GLA_HEADS
GLA_HV = GLA_DV // GLA_HEADS
GLA_GATE_RANK = 16
GLA_GATE_TAU = 16.0
GLA_CHUNK = 64
GLA_MAIN = 2 * GLA_DK + 2 * GLA_DV

DIFF_HEAD_DIM = 128
DIFF_HEADS = D_MODEL // (2 * DIFF_HEAD_DIM)
DIFF_VDIM = 2 * DIFF_HEAD_DIM
DIFF_QK = DIFF_HEADS * 2 * DIFF_HEAD_DIM
DIFF_V = DIFF_HEADS * DIFF_VDIM

D_FF = -(-8 * D_MODEL // (3 * 256)) * 256

LANES = 128
MASK_VALUE = -1e30

BF16 = jnp.bfloat16
F32 = jnp.float32

VMEM_LIMIT = 56 * 1024 * 1024


def _params(n_axes):
    return pltpu.CompilerParams(dimension_semantics=("arbitrary",) * n_axes,
                                vmem_limit_bytes=VMEM_LIMIT)


def _silu(x):
    return x / (1.0 + jnp.exp(-x))


NORM_ROWS = 128


def _fill_normed(x_ref, g_ref, xn_ref):
    gain = g_ref[...]

    def body(i, carry):
        rows = pl.ds(pl.multiple_of(i * NORM_ROWS, NORM_ROWS), NORM_ROWS)
        xv = x_ref[rows, :]
        ms = jnp.mean(xv * xv, axis=-1, keepdims=True)
        xn_ref[rows, :] = ((xv * lax.rsqrt(ms + EPS)) * gain).astype(BF16)
        return carry

    lax.fori_loop(0, x_ref.shape[0] // NORM_ROWS, body, 0)


def _group_norm_store(y, gn_ref, o_ref, scale):
    gain = gn_ref[...]
    for g in range(y.shape[1] // LANES):
        yg = y[:, g * LANES:(g + 1) * LANES]
        ms = jnp.mean(yg * yg, axis=-1, keepdims=True)
        val = (yg * lax.rsqrt(ms + EPS)) * gain
        if scale != 1.0:
            val = val * scale
        o_ref[:, g * LANES:(g + 1) * LANES] = val.astype(o_ref.dtype)


def _norm_proj_kernel(x_ref, g_ref, w_ref, *rest, group_norm, scale, with_gate):
    rest = list(rest)
    w1_ref = rest.pop(0) if with_gate else None
    gn_ref = rest.pop(0) if group_norm else None
    o_ref = rest.pop(0)
    a_ref = rest.pop(0) if with_gate else None
    xn_ref = rest.pop(0)

    @pl.when(pl.program_id(1) == 0)
    def _():
        _fill_normed(x_ref, g_ref, xn_ref)
        if with_gate:
            a_ref[...] = jnp.dot(xn_ref[...], w1_ref[...], preferred_element_type=F32)

    y = jnp.dot(xn_ref[...], w_ref[...], preferred_element_type=F32)
    if group_norm:
        _group_norm_store(y, gn_ref, o_ref, scale)
    else:
        o_ref[...] = y.astype(o_ref.dtype)


def _w_spec(k, bn, layer, col_block0):
    return pl.BlockSpec((None, k, bn), lambda i, j: (layer, 0, j + col_block0))


def _norm_proj(x, gain, w, layer, *, n, bm, bn, out_dtype, col0=0, w1=None, gn_gain=None, scale=1.0):
    m, k = x.shape
    with_gate = w1 is not None
    group_norm = gn_gain is not None
    in_specs = [pl.BlockSpec((bm, k), lambda i, j: (i, 0)),
                pl.BlockSpec((1, k), lambda i, j: (0, 0)),
                _w_spec(k, bn, layer, col0 // bn)]
    args = [x, gain.reshape(1, k), w]
    if with_gate:
        in_specs.append(pl.BlockSpec((k, LANES), lambda i, j: (0, 0)))
        args.append(w1)
    if group_norm:
        in_specs.append(pl.BlockSpec((1, LANES), lambda i, j: (0, 0)))
        args.append(gn_gain.reshape(1, LANES))
    out_shape = [jax.ShapeDtypeStruct((m, n), out_dtype)]
    out_specs = [pl.BlockSpec((bm, bn), lambda i, j: (i, j))]
    if with_gate:
        out_shape.append(jax.ShapeDtypeStruct((m, LANES), F32))
        out_specs.append(pl.BlockSpec((bm, LANES), lambda i, j: (i, 0)))
    res = pl.pallas_call(
        functools.partial(_norm_proj_kernel, group_norm=group_norm, scale=scale, with_gate=with_gate),
        grid=(m // bm, n // bn),
        in_specs=in_specs,
        out_specs=out_specs,
        out_shape=out_shape,
        scratch_shapes=[pltpu.VMEM((bm, k), BF16)],
        compiler_params=_params(2),
        name="norm_proj",
    )(*args)
    return res if with_gate else res[0]


CAST_ROWS = 256


def _cast_tile(src_ref, dst_ref):
    def body(i, carry):
        rows = pl.ds(pl.multiple_of(i * CAST_ROWS, CAST_ROWS), CAST_ROWS)
        dst_ref[rows, :] = src_ref[rows, :].astype(dst_ref.dtype)
        return carry

    lax.fori_loop(0, src_ref.shape[0] // CAST_ROWS, body, 0)


def _ffn_up_kernel(xn_ref, wg_ref, wu_ref, o_ref, wgb_ref, wub_ref):
    @pl.when(pl.program_id(1) == 0)
    def _():
        _cast_tile(wg_ref, wgb_ref)
        _cast_tile(wu_ref, wub_ref)

    xn = xn_ref[...]
    g = jnp.dot(xn, wgb_ref[...], preferred_element_type=F32)
    u = jnp.dot(xn, wub_ref[...], preferred_element_type=F32)
    o_ref[...] = (_silu(g) * u).astype(o_ref.dtype)


def _ffn_up(xn, w_in, layer, *, bm, bn):
    m, k = xn.shape
    nblk = D_FF // bn
    return pl.pallas_call(
        _ffn_up_kernel,
        grid=(nblk, m // bm),
        in_specs=[pl.BlockSpec((bm, k), lambda j, i: (i, 0)),
                  pl.BlockSpec((None, k, bn), lambda j, i: (layer, 0, j)),
                  pl.BlockSpec((None, k, bn), lambda j, i: (layer, 0, j + nblk))],
        out_specs=pl.BlockSpec((bm, bn), lambda j, i: (i, j)),
        out_shape=jax.ShapeDtypeStruct((m, D_FF), BF16),
        scratch_shapes=[pltpu.VMEM((k, bn), BF16), pltpu.VMEM((k, bn), BF16)],
        compiler_params=_params(2),
        name="ffn_up",
    )(xn, w_in, w_in)


def _res_mm_norm_kernel(a_ref, w_ref, r_ref, g_ref, o_ref, xn_ref):
    o_ref[...] = r_ref[...] + jnp.dot(a_ref[...], w_ref[...], preferred_element_type=F32)
    _fill_normed(o_ref, g_ref, xn_ref)


def _res_mm_norm(a, w, layer, res, gain, *, bm):
    m, k = a.shape
    n = w.shape[2]
    row_blk = lambda i: (i, 0)
    return pl.pallas_call(
        _res_mm_norm_kernel,
        grid=(m // bm,),
        in_specs=[pl.BlockSpec((bm, k), row_blk),
                  pl.BlockSpec((None, k, n), lambda i: (layer, 0, 0)),
                  pl.BlockSpec((bm, n), row_blk),
                  pl.BlockSpec((1, n), lambda i: (0, 0))],
        out_specs=[pl.BlockSpec((bm, n), row_blk), pl.BlockSpec((bm, n), row_blk)],
        out_shape=[jax.ShapeDtypeStruct((m, n), F32), jax.ShapeDtypeStruct((m, n), BF16)],
        compiler_params=_params(1),
        name="res_mm_norm",
    )(a, w, res, gain.reshape(1, n))


def _res_mm_kernel(a_ref, w_ref, r_ref, o_ref):
    o_ref[...] = r_ref[...] + jnp.dot(a_ref[...], w_ref[...], preferred_element_type=F32)


def _res_mm(a, w, layer, res, *, bm, bn):
    m, k = a.shape
    n = w.shape[2]
    return pl.pallas_call(
        _res_mm_kernel,
        grid=(m // bm, n // bn),
        in_specs=[pl.BlockSpec((bm, k), lambda i, j: (i, 0)),
                  _w_spec(k, bn, layer, 0),
                  pl.BlockSpec((bm, bn), lambda i, j: (i, j))],
        out_specs=pl.BlockSpec((bm, bn), lambda i, j: (i, j)),
        out_shape=jax.ShapeDtypeStruct((m, n), F32),
        compiler_params=_params(2),
        name="res_mm",
    )(a, w, res)


GLA_ROWS = 512


def _split2(x):
    hi = x.astype(BF16)
    lo = (x - hi.astype(F32)).astype(BF16)
    return hi, lo


def _gla_kernel(p_ref, a_ref, w2_ref, b_ref, hg_ref, o_ref, state_ref):
    c = GLA_CHUNK

    @pl.when(pl.program_id(1) == 0)
    def _():
        state_ref[...] = jnp.zeros_like(state_ref)

    row = lax.broadcasted_iota(jnp.int32, (c, c), 0)
    col = lax.broadcasted_iota(jnp.int32, (c, c), 1)
    causal = row >= col
    tril = causal.astype(BF16)
    head_gain = hg_ref[...]
    tn_dims = (((0,), (0,)), ((), ()))
    nt_dims = (((1,), (1,)), ((), ()))

    def chunk(ci, carry):
        rows = pl.ds(pl.multiple_of(ci * c, c), c)
        heads = range(GLA_HEADS)
        a_low = a_ref[rows, :].astype(BF16)
        z = jnp.dot(a_low, w2_ref[...], preferred_element_type=F32) + b_ref[...]
        log_a = (jnp.minimum(z, 0.0) - jnp.log(1.0 + jnp.exp(-jnp.abs(z)))) / GLA_GATE_TAU
        bcum = sum(jnp.dot(tril, part, preferred_element_type=F32) for part in _split2(log_a))
        b_last = bcum[c - 1:c, :]
        q = p_ref[rows, 0:GLA_DK]
        k = p_ref[rows, GLA_DK:2 * GLA_DK]
        q_dec = ((q * (GLA_HK ** -0.5)) * jnp.exp(bcum)).astype(BF16)
        k_inv = (k * jnp.exp(-bcum)).astype(BF16)
        k_end = (k * jnp.exp(b_last - bcum)).astype(BF16)
        chunk_decay = jnp.exp(b_last)
        v_b = p_ref[rows, 2 * GLA_DK:2 * GLA_DK + GLA_DV].astype(BF16)
        hk = [slice(h * GLA_HK, (h + 1) * GLA_HK) for h in heads]
        hv = [slice(h * GLA_HV, (h + 1) * GLA_HV) for h in heads]
        states = [state_ref[h] for h in heads]
        o_inter = [jnp.dot(q_dec[:, hk[h]], states[h].astype(BF16), preferred_element_type=F32) for h in heads]
        kv = [lax.dot_general(k_end[:, hk[h]], v_b[:, hv[h]], tn_dims, preferred_element_type=F32)
              for h in heads]
        attn = [lax.dot_general(q_dec[:, hk[h]], k_inv[:, hk[h]], nt_dims, preferred_element_type=F32)
                for h in heads]
        o_intra = [jnp.dot(jnp.where(causal, attn[h], 0.0).astype(BF16), v_b[:, hv[h]],
                           preferred_element_type=F32) for h in heads]
        for h in heads:
            decay = jnp.broadcast_to(chunk_decay[:, hk[h]], (LANES, GLA_HK)).T
            state_ref[h] = states[h] * jnp.concatenate([decay] * (GLA_HV // LANES), axis=1) + kv[h]
            o = o_intra[h] + o_inter[h]
            ms = jnp.mean(o * o, axis=-1, keepdims=True)
            o = (o * lax.rsqrt(ms + EPS)) * head_gain
            r = p_ref[rows, 2 * GLA_DK + GLA_DV + h * GLA_HV:2 * GLA_DK + GLA_DV + (h + 1) * GLA_HV]
            o_ref[rows, hv[h]] = (o * _silu(r)).astype(o_ref.dtype)
        return carry

    lax.fori_loop(0, GLA_ROWS // c, chunk, 0, unroll=2)


def _gla_core(proj, a_low, w2, b_alpha, head_gain):
    nblk = SEQ // GLA_ROWS
    return pl.pallas_call(
        _gla_kernel,
        grid=(BATCH, nblk),
        in_specs=[pl.BlockSpec((GLA_ROWS, GLA_MAIN), lambda b, n: (b * nblk + n, 0)),
                  pl.BlockSpec((GLA_ROWS, LANES), lambda b, n: (b * nblk + n, 0)),
                  pl.BlockSpec((LANES, GLA_DK), lambda b, n: (0, 0)),
                  pl.BlockSpec((1, GLA_DK), lambda b, n: (0, 0)),
                  pl.BlockSpec((1, GLA_HV), lambda b, n: (0, 0))],
        out_specs=pl.BlockSpec((GLA_ROWS, GLA_DV), lambda b, n: (b * nblk + n, 0)),
        out_shape=jax.ShapeDtypeStruct((TOKENS, GLA_DV), BF16),
        scratch_shapes=[pltpu.VMEM((GLA_HEADS, GLA_HK, GLA_HV), F32)],
        compiler_params=_params(2),
        name="gla_core",
    )(proj, a_low, w2, b_alpha.reshape(1, GLA_DK), head_gain.reshape(1, GLA_HV))


ATT_TQ = 512
ATT_TK = 512
LOG2E = math.log2(math.e)


def _lane_tiles(x):
    return [x[:, i * LANES:(i + 1) * LANES] for i in range(x.shape[1] // LANES)]


def _attn_kernel(slope_ref, q_ref, k_ref, v_ref, lq1_ref, lk1_ref, lq2_ref, lk2_ref, hg_ref, o_ref,
                 bias_ref, biasd_ref, m_ref, l_ref, acc_ref, *, lambda_init):
    tq, tk, dh = ATT_TQ, ATT_TK, DIFF_HEAD_DIM
    nt_dims = (((1,), (1,)), ((), ()))
    slope2 = slope_ref[0][:, :1] * LOG2E
    row = lax.broadcasted_iota(jnp.int32, (tq, tk), 0)
    col = lax.broadcasted_iota(jnp.int32, (tq, tk), 1)
    bias = -slope2 * (row - col).astype(F32)
    bias_ref[...] = bias
    biasd_ref[...] = jnp.where(row >= col, bias, MASK_VALUE)
    lam = (jnp.exp(jnp.sum(lq1_ref[...] * lk1_ref[...], axis=-1, keepdims=True))
           - jnp.exp(jnp.sum(lq2_ref[...] * lk2_ref[...], axis=-1, keepdims=True))
           + lambda_init)
    head_gain = hg_ref[...]

    def kstep(q0, k0, krows, b_ref):
        if isinstance(k0, int):
            shift = -slope2 * float(q0 - k0)
        else:
            shift = -slope2 * (q0 - k0).astype(F32)
        v_blk = v_ref[krows, :]
        for comp in range(2):
            cs = slice(comp * dh, (comp + 1) * dh)
            t = lax.dot_general(q_ref[q0:q0 + tq, cs], k_ref[krows, cs], nt_dims,
                                preferred_element_type=F32) + b_ref[...]
            tiles = _lane_tiles(t)
            tmax = functools.reduce(jnp.maximum, tiles)
            m_prev = m_ref[comp]
            m_new = jnp.maximum(m_prev, jnp.max(tmax, axis=-1, keepdims=True) + shift)
            alpha = jnp.exp2(m_prev - m_new)
            off = shift - m_new
            p_tiles = [jnp.exp2(ti + off) for ti in tiles]
            l_ref[comp] = alpha * l_ref[comp] + functools.reduce(jnp.add, p_tiles)
            p = jnp.concatenate([pt.astype(BF16) for pt in p_tiles], axis=1)
            pv = jnp.dot(p, v_blk, preferred_element_type=F32)
            acc_ref[comp] = jnp.concatenate([alpha] * (DIFF_VDIM // LANES), axis=1) * acc_ref[comp] + pv
            m_ref[comp] = m_new

    for qi in range(SEQ // tq):
        q0 = qi * tq
        m_ref[...] = jnp.full(m_ref.shape, MASK_VALUE, F32)
        l_ref[...] = jnp.zeros_like(l_ref)
        acc_ref[...] = jnp.zeros_like(acc_ref)

        for kj in range(qi):
            kstep(q0, kj * tk, slice(kj * tk, (kj + 1) * tk), bias_ref)
        kstep(q0, q0, slice(q0, q0 + tk), biasd_ref)

        l0 = jnp.sum(l_ref[0], axis=-1, keepdims=True)
        l1 = jnp.sum(l_ref[1], axis=-1, keepdims=True)
        o = acc_ref[0] / l0 - lam * (acc_ref[1] / l1)
        ms = jnp.mean(o * o, axis=-1, keepdims=True)
        o = ((o * lax.rsqrt(ms + EPS)) * head_gain) * (1.0 - lambda_init)
        o_ref[q0:q0 + tq, :] = o.astype(o_ref.dtype)


def _diff_attn_core(q, k, v, slopes, lq1, lk1, lq2, lk2, head_gain, lambda_init):
    tq, tk = ATT_TQ, ATT_TK
    assert tq == tk
    blk = pl.BlockSpec((SEQ, DIFF_VDIM), lambda b, h: (b, h))
    vec = pl.BlockSpec((1, DIFF_HEAD_DIM), lambda b, h: (0, 0))
    return pl.pallas_call(
        functools.partial(_attn_kernel, lambda_init=lambda_init),
        grid=(BATCH, DIFF_HEADS),
        in_specs=[pl.BlockSpec((1, 1, LANES), lambda b, h: (h, 0, 0)),
                  blk, blk, blk, vec, vec, vec, vec,
                  pl.BlockSpec((1, DIFF_VDIM), lambda b, h: (0, 0))],
        out_specs=blk,
        out_shape=jax.ShapeDtypeStruct((TOKENS, DIFF_V), BF16),
        scratch_shapes=[pltpu.VMEM((tq, tk), F32),
                        pltpu.VMEM((tq, tk), F32),
                        pltpu.VMEM((2, tq, LANES), F32),
                        pltpu.VMEM((2, tq, LANES), F32),
                        pltpu.VMEM((2, tq, DIFF_VDIM), F32)],
        compiler_params=_params(2),
        name="diff_attn",
    )(slopes, q, k, v,
      lq1.reshape(1, DIFF_HEAD_DIM), lk1.reshape(1, DIFF_HEAD_DIM),
      lq2.reshape(1, DIFF_HEAD_DIM), lk2.reshape(1, DIFF_HEAD_DIM),
      head_gain.reshape(1, DIFF_VDIM))


CAST_BLOCK_BYTES = 8 * 1024 * 1024


def _cast_kernel(x_ref, o_ref):
    o_ref[...] = x_ref[...].astype(o_ref.dtype)


def _to_bf16(w):
    l, r, c = w.shape
    bc = next((t for t in (2048, 1024) if c % t == 0), c)
    br = next(t for t in (2048, 1024, 512, 256, 128) if r % t == 0 and t * bc * 4 <= CAST_BLOCK_BYTES)
    spec = pl.BlockSpec((None, br, bc), lambda n, i, j: (n, i, j))
    return pl.pallas_call(
        _cast_kernel,
        grid=(l, r // br, c // bc),
        in_specs=[spec],
        out_specs=spec,
        out_shape=jax.ShapeDtypeStruct((l, r, c), BF16),
        compiler_params=_params(3),
        name="cast_bf16",
    )(w)


def kernel(x, gla_attn_norm, gla_w_in, gla_w_alpha2, gla_b_alpha, gla_head_norm, gla_w_out, kv_norm, w_kv, k_norm, diff_attn_norm, diff_w_q, diff_q_norm, diff_lambda_q1, diff_lambda_k1, diff_lambda_q2, diff_lambda_k2, diff_head_norm, diff_w_out, ffn_norm, ffn_w_in, ffn_w_out):
    xt = x.reshape(TOKENS, D_MODEL)
    pad = LANES - GLA_GATE_RANK
    slopes = (2.0 ** (-8.0 * jnp.arange(1, DIFF_HEADS + 1, dtype=F32) / DIFF_HEADS))
    slopes = jnp.broadcast_to(slopes[:, None, None], (DIFF_HEADS, 1, LANES))

    gla_in_b = _to_bf16(gla_w_in)
    gla_out_b = _to_bf16(gla_w_out)
    kv_b = _to_bf16(w_kv[None])
    wq_b = _to_bf16(diff_w_q)
    wo_b = _to_bf16(diff_w_out)
    ffn_out_b = _to_bf16(ffn_w_out)

    k_shared = v_shared = None
    for l in range(DEPTH):
        if l < N_A_LAYERS:
            i = l
            w1 = jnp.pad(gla_in_b[i, :, GLA_MAIN:], ((0, 0), (0, pad)))
            w2 = jnp.pad(gla_w_alpha2[i], ((0, pad), (0, 0))).astype(BF16)
            proj, a_low = _norm_proj(xt, gla_attn_norm[i], gla_in_b, i, n=GLA_MAIN, bm=1024, bn=1024,
                                     out_dtype=F32, w1=w1)
            o = _gla_core(proj, a_low, w2, gla_b_alpha[i], gla_head_norm[i])
            xt, xn = _res_mm_norm(o, gla_out_b, i, xt, ffn_norm[l], bm=512)
        else:
            if l == N_A_LAYERS:
                k_shared = _norm_proj(xt, kv_norm, kv_b, 0, n=DIFF_QK, bm=1024, bn=1024,
                                      out_dtype=BF16, gn_gain=k_norm)
                v_shared = _norm_proj(xt, kv_norm, kv_b, 0, n=DIFF_V, col0=DIFF_QK, bm=1024, bn=1024,
                                      out_dtype=BF16)
            j = l - N_A_LAYERS
            lambda_init = 0.8 - 0.6 * math.exp(-0.3 * l)
            q = _norm_proj(xt, diff_attn_norm[j], wq_b, j, n=DIFF_QK, bm=1024, bn=1024,
                           out_dtype=BF16, gn_gain=diff_q_norm[j], scale=DIFF_HEAD_DIM ** -0.5 * LOG2E)
            o = _diff_attn_core(q, k_shared, v_shared, slopes, diff_lambda_q1[j], diff_lambda_k1[j],
                                diff_lambda_q2[j], diff_lambda_k2[j], diff_head_norm[j], lambda_init)
            xt, xn = _res_mm_norm(o, wo_b, j, xt, ffn_norm[l], bm=512)
        act = _ffn_up(xn, ffn_w_in, l, bm=1024, bn=512)
        xt = _res_mm(act, ffn_out_b, l, xt, bm=1024, bn=512)
    return xt.reshape(BATCH, SEQ, D_MODEL)
```

```python
import functools
import math

import jax
import jax.numpy as jnp
from jax import lax
from jax.experimental import pallas as pl
from jax.experimental.pallas import tpu as pltpu

D_MODEL = 2048
BATCH = 4
SEQ = 2048
DEPTH = 4
TOKENS = BATCH * SEQ
N_A_LAYERS = DEPTH // 2
N_B_LAYERS = DEPTH - N_A_LAYERS
EPS = 1e-6

GLA_HEADS = 4
GLA_DK = D_MODEL // 2
GLA_DV = D_MODEL
GLA_HK = GLA_DK // GLA_HEADS
GLA_HV = GLA_DV // GLA_HEADS
GLA_GATE_RANK = 16
GLA_GATE_TAU = 16.0
GLA_CHUNK = 64
GLA_MAIN = 2 * GLA_DK + 2 * GLA_DV

DIFF_HEAD_DIM = 128
DIFF_HEADS = D_MODEL // (2 * DIFF_HEAD_DIM)
DIFF_VDIM = 2 * DIFF_HEAD_DIM
DIFF_QK = DIFF_HEADS * 2 * DIFF_HEAD_DIM
DIFF_V = DIFF_HEADS * DIFF_VDIM

D_FF = -(-8 * D_MODEL // (3 * 256)) * 256

LANES = 128
MASK_VALUE = -1e30

BF16 = jnp.bfloat16
F32 = jnp.float32

VMEM_LIMIT = 56 * 1024 * 1024


def _params(n_axes):
    return pltpu.CompilerParams(dimension_semantics=("arbitrary",) * n_axes,
                                vmem_limit_bytes=VMEM_LIMIT)


def _silu(x):
    return x / (1.0 + jnp.exp(-x))


NORM_ROWS = 128


def _fill_normed(x_ref, g_ref, xn_ref):
    gain = g_ref[...]

    def body(i, carry):
        rows = pl.ds(pl.multiple_of(i * NORM_ROWS, NORM_ROWS), NORM_ROWS)
        xv = x_ref[rows, :]
        ms = jnp.mean(xv * xv, axis=-1, keepdims=True)
        xn_ref[rows, :] = ((xv * lax.rsqrt(ms + EPS)) * gain).astype(BF16)
        return carry

    lax.fori_loop(0, x_ref.shape[0] // NORM_ROWS, body, 0)


def _group_norm_store(y, gn_ref, o_ref, scale):
    gain = gn_ref[...]
    for g in range(y.shape[1] // LANES):
        yg = y[:, g * LANES:(g + 1) * LANES]
        ms = jnp.mean(yg * yg, axis=-1, keepdims=True)
        val = (yg * lax.rsqrt(ms + EPS)) * gain
        if scale != 1.0:
            val = val * scale
        o_ref[:, g * LANES:(g + 1) * LANES] = val.astype(o_ref.dtype)


NT_DIMS = (((1,), (1,)), ((), ()))


def _norm_proj_kernel(x_ref, g_ref, w_ref, *rest, group_norm, scale, gate_rank, w_transposed):
    rest = list(rest)
    w1_ref = rest.pop(0) if gate_rank else None
    gn_ref = rest.pop(0) if group_norm else None
    o_ref = rest.pop(0)
    a_ref = rest.pop(0) if gate_rank else None
    xn_ref = rest.pop(0)

    def project(w_blk):
        if w_transposed:
            return lax.dot_general(xn_ref[...], w_blk, NT_DIMS, preferred_element_type=F32)
        return jnp.dot(xn_ref[...], w_blk, preferred_element_type=F32)

    @pl.when(pl.program_id(1) == 0)
    def _():
        _fill_normed(x_ref, g_ref, xn_ref)
        if gate_rank:
            a_ref[:, :gate_rank] = project(w1_ref[...])
            a_ref[:, gate_rank:] = jnp.zeros((a_ref.shape[0], a_ref.shape[1] - gate_rank), F32)

    y = project(w_ref[...])
    if group_norm:
        _group_norm_store(y, gn_ref, o_ref, scale)
    else:
        o_ref[...] = y.astype(o_ref.dtype)


def _w_spec(k, bn, layer, col_block0):
    return pl.BlockSpec((None, k, bn), lambda i, j: (layer, 0, j + col_block0))


def _norm_proj(x, gain, w, layer, *, n, bm, bn, out_dtype, col0=0, gate_rank=0, gn_gain=None, scale=1.0,
               w_transposed=False):
    m, k = x.shape
    group_norm = gn_gain is not None
    assert not gate_rank or w_transposed
    if w_transposed:
        w_spec = pl.BlockSpec((None, bn, k), lambda i, j: (layer, j + col0 // bn, 0))
    else:
        w_spec = _w_spec(k, bn, layer, col0 // bn)
    in_specs = [pl.BlockSpec((bm, k), lambda i, j: (i, 0)),
                pl.BlockSpec((1, k), lambda i, j: (0, 0)),
                w_spec]
    args = [x, gain.reshape(1, k), w]
    if gate_rank:
        in_specs.append(pl.BlockSpec((None, gate_rank, k), lambda i, j: (layer, (col0 + n) // gate_rank, 0)))
        args.append(w)
    if group_norm:
        in_specs.append(pl.BlockSpec((1, LANES), lambda i, j: (0, 0)))
        args.append(gn_gain.reshape(1, LANES))
    out_shape = [jax.ShapeDtypeStruct((m, n), out_dtype)]
    out_specs = [pl.BlockSpec((bm, bn), lambda i, j: (i, j))]
    if gate_rank:
        out_shape.append(jax.ShapeDtypeStruct((m, LANES), F32))
        out_specs.append(pl.BlockSpec((bm, LANES), lambda i, j: (i, 0)))
    res = pl.pallas_call(
        functools.partial(_norm_proj_kernel, group_norm=group_norm, scale=scale, gate_rank=gate_rank,
                          w_transposed=w_transposed),
        grid=(m // bm, n // bn),
        in_specs=in_specs,
        out_specs=out_specs,
        out_shape=out_shape,
        scratch_shapes=[pltpu.VMEM((bm, k), BF16)],
        compiler_params=_params(2),
        name="norm_proj",
    )(*args)
    return res if gate_rank else res[0]


CAST_ROWS = 256


def _cast_tile(src_ref, dst_ref):
    def body(i, carry):
        rows = pl.ds(pl.multiple_of(i * CAST_ROWS, CAST_ROWS), CAST_ROWS)
        dst_ref[rows, :] = src_ref[rows, :].astype(dst_ref.dtype)
        return carry

    lax.fori_loop(0, src_ref.shape[0] // CAST_ROWS, body, 0)


def _ffn_up_kernel(xn_ref, wg_ref, wu_ref, o_ref, wgb_ref, wub_ref):
    @pl.when(pl.program_id(1) == 0)
    def _():
        _cast_tile(wg_ref, wgb_ref)
        _cast_tile(wu_ref, wub_ref)

    xn = xn_ref[...]
    g = jnp.dot(xn, wgb_ref[...], preferred_element_type=F32)
    u = jnp.dot(xn, wub_ref[...], preferred_element_type=F32)
    o_ref[...] = (_silu(g) * u).astype(o_ref.dtype)


def _ffn_up(xn, w_in, layer, *, bm, bn):
    m, k = xn.shape
    nblk = D_FF // bn
    return pl.pallas_call(
        _ffn_up_kernel,
        grid=(nblk, m // bm),
        in_specs=[pl.BlockSpec((bm, k), lambda j, i: (i, 0)),
                  pl.BlockSpec((None, k, bn), lambda j, i: (layer, 0, j)),
                  pl.BlockSpec((None, k, bn), lambda j, i: (layer, 0, j + nblk))],
        out_specs=pl.BlockSpec((bm, bn), lambda j, i: (i, j)),
        out_shape=jax.ShapeDtypeStruct((m, D_FF), BF16),
        scratch_shapes=[pltpu.VMEM((k, bn), BF16), pltpu.VMEM((k, bn), BF16)],
        compiler_params=_params(2),
        name="ffn_up",
    )(xn, w_in, w_in)


def _res_mm_norm_kernel(a_ref, w_ref, r_ref, g_ref, o_ref, xn_ref):
    o_ref[...] = r_ref[...] + jnp.dot(a_ref[...], w_ref[...], preferred_element_type=F32)
    _fill_normed(o_ref, g_ref, xn_ref)


def _res_mm_norm(a, w, layer, res, gain, *, bm):
    m, k = a.shape
    n = w.shape[2]
    row_blk = lambda i: (i, 0)
    return pl.pallas_call(
        _res_mm_norm_kernel,
        grid=(m // bm,),
        in_specs=[pl.BlockSpec((bm, k), row_blk),
                  pl.BlockSpec((None, k, n), lambda i: (layer, 0, 0)),
                  pl.BlockSpec((bm, n), row_blk),
                  pl.BlockSpec((1, n), lambda i: (0, 0))],
        out_specs=[pl.BlockSpec((bm, n), row_blk), pl.BlockSpec((bm, n), row_blk)],
        out_shape=[jax.ShapeDtypeStruct((m, n), F32), jax.ShapeDtypeStruct((m, n), BF16)],
        compiler_params=_params(1),
        name="res_mm_norm",
    )(a, w, res, gain.reshape(1, n))


def _res_mm_kernel(a_ref, w_ref, r_ref, o_ref):
    o_ref[...] = r_ref[...] + jnp.dot(a_ref[...], w_ref[...], preferred_element_type=F32)


def _res_mm(a, w, layer, res, *, bm, bn):
    m, k = a.shape
    n = w.shape[2]
    return pl.pallas_call(
        _res_mm_kernel,
        grid=(m // bm, n // bn),
        in_specs=[pl.BlockSpec((bm, k), lambda i, j: (i, 0)),
                  _w_spec(k, bn, layer, 0),
                  pl.BlockSpec((bm, bn), lambda i, j: (i, j))],
        out_specs=pl.BlockSpec((bm, bn), lambda i, j: (i, j)),
        out_shape=jax.ShapeDtypeStruct((m, n), F32),
        compiler_params=_params(2),
        name="res_mm",
    )(a, w, res)


GLA_ROWS = 512


def _split2(x):
    hi = x.astype(BF16)
    lo = (x - hi.astype(F32)).astype(BF16)
    return hi, lo


def _gla_kernel(p_ref, a_ref, w2_ref, b_ref, hg_ref, o_ref, state_ref):
    c = GLA_CHUNK

    @pl.when(pl.program_id(1) == 0)
    def _():
        state_ref[...] = jnp.zeros_like(state_ref)

    row = lax.broadcasted_iota(jnp.int32, (c, c), 0)
    col = lax.broadcasted_iota(jnp.int32, (c, c), 1)
    causal = row >= col
    tril = causal.astype(BF16)
    head_gain = hg_ref[...]
    tn_dims = (((0,), (0,)), ((), ()))
    nt_dims = (((1,), (1,)), ((), ()))

    def chunk(ci, carry):
        rows = pl.ds(pl.multiple_of(ci * c, c), c)
        heads = range(GLA_HEADS)
        a_low = a_ref[rows, :].astype(BF16)
        z = jnp.dot(a_low, w2_ref[...], preferred_element_type=F32) + b_ref[...]
        log_a = (jnp.minimum(z, 0.0) - jnp.log(1.0 + jnp.exp(-jnp.abs(z)))) / GLA_GATE_TAU
        bcum = sum(jnp.dot(tril, part, preferred_element_type=F32) for part in _split2(log_a))
        b_last = bcum[c - 1:c, :]
        q = p_ref[rows, 0:GLA_DK]
        k = p_ref[rows, GLA_DK:2 * GLA_DK]
        q_dec = ((q * (GLA_HK ** -0.5)) * jnp.exp(bcum)).astype(BF16)
        k_inv = (k * jnp.exp(-bcum)).astype(BF16)
        k_end = (k * jnp.exp(b_last - bcum)).astype(BF16)
        chunk_decay = jnp.exp(b_last)
        v_b = p_ref[rows, 2 * GLA_DK:2 * GLA_DK + GLA_DV].astype(BF16)
        hk = [slice(h * GLA_HK, (h + 1) * GLA_HK) for h in heads]
        hv = [slice(h * GLA_HV, (h + 1) * GLA_HV) for h in heads]
        states = [state_ref[h] for h in heads]
        o_inter = [jnp.dot(q_dec[:, hk[h]], states[h].astype(BF16), preferred_element_type=F32) for h in heads]
        kv = [lax.dot_general(k_end[:, hk[h]], v_b[:, hv[h]], tn_dims, preferred_element_type=F32)
              for h in heads]
        attn = [lax.dot_general(q_dec[:, hk[h]], k_inv[:, hk[h]], nt_dims, preferred_element_type=F32)
                for h in heads]
        o_intra = [jnp.dot(jnp.where(causal, attn[h], 0.0).astype(BF16), v_b[:, hv[h]],
                           preferred_element_type=F32) for h in heads]
        for h in heads:
            decay = jnp.broadcast_to(chunk_decay[:, hk[h]], (LANES, GLA_HK)).T
            state_ref[h] = states[h] * jnp.concatenate([decay] * (GLA_HV // LANES), axis=1) + kv[h]
            o = o_intra[h] + o_inter[h]
            ms = jnp.mean(o * o, axis=-1, keepdims=True)
            o = (o * lax.rsqrt(ms + EPS)) * head_gain
            r = p_ref[rows, 2 * GLA_DK + GLA_DV + h * GLA_HV:2 * GLA_DK + GLA_DV + (h + 1) * GLA_HV]
            o_ref[rows, hv[h]] = (o * _silu(r)).astype(o_ref.dtype)
        return carry

    lax.fori_loop(0, GLA_ROWS // c, chunk, 0, unroll=2)


def _gla_core(proj, a_low, w2, b_alpha, head_gain):
    nblk = SEQ // GLA_ROWS
    return pl.pallas_call(
        _gla_kernel,
        grid=(BATCH, nblk),
        in_specs=[pl.BlockSpec((GLA_ROWS, GLA_MAIN), lambda b, n: (b * nblk + n, 0)),
                  pl.BlockSpec((GLA_ROWS, LANES), lambda b, n: (b * nblk + n, 0)),
                  pl.BlockSpec((LANES, GLA_DK), lambda b, n: (0, 0)),
                  pl.BlockSpec((1, GLA_DK), lambda b, n: (0, 0)),
                  pl.BlockSpec((1, GLA_HV), lambda b, n: (0, 0))],
        out_specs=pl.BlockSpec((GLA_ROWS, GLA_DV), lambda b, n: (b * nblk + n, 0)),
        out_shape=jax.ShapeDtypeStruct((TOKENS, GLA_DV), BF16),
        scratch_shapes=[pltpu.VMEM((GLA_HEADS, GLA_HK, GLA_HV), F32)],
        compiler_params=_params(2),
        name="gla_core",
    )(proj, a_low, w2, b_alpha.reshape(1, GLA_DK), head_gain.reshape(1, GLA_HV))


ATT_TQ = 512
ATT_TK = 512
LOG2E = math.log2(math.e)


def _lane_tiles(x):
    return [x[:, i * LANES:(i + 1) * LANES] for i in range(x.shape[1] // LANES)]


def _attn_kernel(slope_ref, q_ref, k_ref, v_ref, lq1_ref, lk1_ref, lq2_ref, lk2_ref, hg_ref, o_ref,
                 bias_ref, biasd_ref, m_ref, l_ref, acc_ref, *, lambda_init):
    tq, tk, dh = ATT_TQ, ATT_TK, DIFF_HEAD_DIM
    nt_dims = (((1,), (1,)), ((), ()))
    slope2 = slope_ref[0][:, :1] * LOG2E
    row = lax.broadcasted_iota(jnp.int32, (tq, tk), 0)
    col = lax.broadcasted_iota(jnp.int32, (tq, tk), 1)
    bias = -slope2 * (row - col).astype(F32)
    bias_ref[...] = bias
    biasd_ref[...] = jnp.where(row >= col, bias, MASK_VALUE)
    lam = (jnp.exp(jnp.sum(lq1_ref[...] * lk1_ref[...], axis=-1, keepdims=True))
           - jnp.exp(jnp.sum(lq2_ref[...] * lk2_ref[...], axis=-1, keepdims=True))
           + lambda_init)
    head_gain = hg_ref[...]

    def kstep(q0, k0, krows, b_ref):
        if isinstance(k0, int):
            shift = -slope2 * float(q0 - k0)
        else:
            shift = -slope2 * (q0 - k0).astype(F32)
        v_blk = v_ref[krows, :]
        for comp in range(2):
            cs = slice(comp * dh, (comp + 1) * dh)
            t = lax.dot_general(q_ref[q0:q0 + tq, cs], k_ref[krows, cs], nt_dims,
                                preferred_element_type=F32) + b_ref[...]
            tiles = _lane_tiles(t)
            tmax = functools.reduce(jnp.maximum, tiles)
            m_prev = m_ref[comp]
            m_new = jnp.maximum(m_prev, jnp.max(tmax, axis=-1, keepdims=True) + shift)
            alpha = jnp.exp2(m_prev - m_new)
            off = shift - m_new
            p_tiles = [jnp.exp2(ti + off) for ti in tiles]
            l_ref[comp] = alpha * l_ref[comp] + functools.reduce(jnp.add, p_tiles)
            p = jnp.concatenate([pt.astype(BF16) for pt in p_tiles], axis=1)
            pv = jnp.dot(p, v_blk, preferred_element_type=F32)
            acc_ref[comp] = jnp.concatenate([alpha] * (DIFF_VDIM // LANES), axis=1) * acc_ref[comp] + pv
            m_ref[comp] = m_new

    for qi in range(SEQ // tq):
        q0 = qi * tq
        m_ref[...] = jnp.full(m_ref.shape, MASK_VALUE, F32)
        l_ref[...] = jnp.zeros_like(l_ref)
        acc_ref[...] = jnp.zeros_like(acc_ref)

        for kj in range(qi):
            kstep(q0, kj * tk, slice(kj * tk, (kj + 1) * tk), bias_ref)
        kstep(q0, q0, slice(q0, q0 + tk), biasd_ref)

        l0 = jnp.sum(l_ref[0], axis=-1, keepdims=True)
        l1 = jnp.sum(l_ref[1], axis=-1, keepdims=True)
        o = acc_ref[0] / l0 - lam * (acc_ref[1] / l1)
        ms = jnp.mean(o * o, axis=-1, keepdims=True)
        o = ((o * lax.rsqrt(ms + EPS)) * head_gain) * (1.0 - lambda_init)
        o_ref[q0:q0 + tq, :] = o.astype(o_ref.dtype)


def _diff_attn_core(q, k, v, slopes, lq1, lk1, lq2, lk2, head_gain, lambda_init):
    tq, tk = ATT_TQ, ATT_TK
    assert tq == tk
    blk = pl.BlockSpec((SEQ, DIFF_VDIM), lambda b, h: (b, h))
    vec = pl.BlockSpec((1, DIFF_HEAD_DIM), lambda b, h: (0, 0))
    return pl.pallas_call(
        functools.partial(_attn_kernel, lambda_init=lambda_init),
        grid=(BATCH, DIFF_HEADS),
        in_specs=[pl.BlockSpec((1, 1, LANES), lambda b, h: (h, 0, 0)),
                  blk, blk, blk, vec, vec, vec, vec,
                  pl.BlockSpec((1, DIFF_VDIM), lambda b, h: (0, 0))],
        out_specs=blk,
        out_shape=jax.ShapeDtypeStruct((TOKENS, DIFF_V), BF16),
        scratch_shapes=[pltpu.VMEM((tq, tk), F32),
                        pltpu.VMEM((tq, tk), F32),
                        pltpu.VMEM((2, tq, LANES), F32),
                        pltpu.VMEM((2, tq, LANES), F32),
                        pltpu.VMEM((2, tq, DIFF_VDIM), F32)],
        compiler_params=_params(2),
        name="diff_attn",
    )(slopes, q, k, v,
      lq1.reshape(1, DIFF_HEAD_DIM), lk1.reshape(1, DIFF_HEAD_DIM),
      lq2.reshape(1, DIFF_HEAD_DIM), lk2.reshape(1, DIFF_HEAD_DIM),
      head_gain.reshape(1, DIFF_VDIM))


CAST_BLOCK_BYTES = 8 * 1024 * 1024


def _cast_kernel(x_ref, o_ref):
    o_ref[...] = x_ref[...].astype(o_ref.dtype)


def _to_bf16(w):
    l, r, c = w.shape
    bc = next((t for t in (2048, 1024) if c % t == 0), c)
    br = max(t for t in range(16, r + 1, 16) if r % t == 0 and t * bc * 4 <= CAST_BLOCK_BYTES)
    spec = pl.BlockSpec((None, br, bc), lambda n, i, j: (n, i, j))
    return pl.pallas_call(
        _cast_kernel,
        grid=(l, r // br, c // bc),
        in_specs=[spec],
        out_specs=spec,
        out_shape=jax.ShapeDtypeStruct((l, r, c), BF16),
        compiler_params=_params(3),
        name="cast_bf16",
    )(w)


def kernel(x, gla_attn_norm, gla_w_in, gla_w_alpha2, gla_b_alpha, gla_head_norm, gla_w_out, kv_norm, w_kv, k_norm, diff_attn_norm, diff_w_q, diff_q_norm, diff_lambda_q1, diff_lambda_k1, diff_lambda_q2, diff_lambda_k2, diff_head_norm, diff_w_out, ffn_norm, ffn_w_in, ffn_w_out):
    xt = x.reshape(TOKENS, D_MODEL)
    pad = LANES - GLA_GATE_RANK
    slopes = (2.0 ** (-8.0 * jnp.arange(1, DIFF_HEADS + 1, dtype=F32) / DIFF_HEADS))
    slopes = jnp.broadcast_to(slopes[:, None, None], (DIFF_HEADS, 1, LANES))

    gla_in_b = _to_bf16(jnp.swapaxes(gla_w_in, 1, 2))
    gla_out_b = _to_bf16(gla_w_out)
    kv_b = _to_bf16(w_kv[None])
    wq_b = _to_bf16(diff_w_q)
    wo_b = _to_bf16(diff_w_out)
    ffn_out_b = _to_bf16(ffn_w_out)

    k_shared = v_shared = None
    for l in range(DEPTH):
        if l < N_A_LAYERS:
            i = l
            w2 = jnp.pad(gla_w_alpha2[i], ((0, pad), (0, 0))).astype(BF16)
            proj, a_low = _norm_proj(xt, gla_attn_norm[i], gla_in_b, i, n=GLA_MAIN, bm=1024, bn=1024,
                                     out_dtype=F32, gate_rank=GLA_GATE_RANK, w_transposed=True)
            o = _gla_core(proj, a_low, w2, gla_b_alpha[i], gla_head_norm[i])
            xt, xn = _res_mm_norm(o, gla_out_b, i, xt, ffn_norm[l], bm=512)
        else:
            if l == N_A_LAYERS:
                k_shared = _norm_proj(xt, kv_norm, kv_b, 0, n=DIFF_QK, bm=1024, bn=1024,
                                      out_dtype=BF16, gn_gain=k_norm)
                v_shared = _norm_proj(xt, kv_norm, kv_b, 0, n=DIFF_V, col0=DIFF_QK, bm=1024, bn=1024,
                                      out_dtype=BF16)
            j = l - N_A_LAYERS
            lambda_init = 0.8 - 0.6 * math.exp(-0.3 * l)
            q = _norm_proj(xt, diff_attn_norm[j], wq_b, j, n=DIFF_QK, bm=1024, bn=1024,
                           out_dtype=BF16, gn_gain=diff_q_norm[j], scale=DIFF_HEAD_DIM ** -0.5 * LOG2E)
            o = _diff_attn_core(q, k_shared, v_shared, slopes, diff_lambda_q1[j], diff_lambda_k1[j],
                                diff_lambda_q2[j], diff_lambda_k2[j], diff_head_norm[j], lambda_init)
            xt, xn = _res_mm_norm(o, wo_b, j, xt, ffn_norm[l], bm=512)
        act = _ffn_up(xn, ffn_w_in, l, bm=2048, bn=512)
        xt = _res_mm(act, ffn_out_b, l, xt, bm=1024, bn=512)
    return xt.reshape(BATCH, SEQ, D_MODEL)
```

```python
import functools
import math

import jax
import jax.numpy as jnp
from jax import lax
from jax.experimental import pallas as pl
from jax.experimental.pallas import tpu as pltpu

D_MODEL = 2048
BATCH = 4
SEQ = 2048
DEPTH = 4
TOKENS = BATCH * SEQ
N_A_LAYERS = DEPTH // 2
N_B_LAYERS = DEPTH - N_A_LAYERS
EPS = 1e-6

GLA_HEADS = 4
GLA_DK = D_MODEL // 2
GLA_DV = D_MODEL
GLA_HK = GLA_DK // GLA_HEADS
GLA_HV = GLA_DV // GLA_HEADS
GLA_GATE_RANK = 16
GLA_GATE_TAU = 16.0
GLA_CHUNK = 64
GLA_MAIN = 2 * GLA_DK + 2 * GLA_DV

DIFF_HEAD_DIM = 128
DIFF_HEADS = D_MODEL // (2 * DIFF_HEAD_DIM)
DIFF_VDIM = 2 * DIFF_HEAD_DIM
DIFF_QK = DIFF_HEADS * 2 * DIFF_HEAD_DIM
DIFF_V = DIFF_HEADS * DIFF_VDIM

D_FF = -(-8 * D_MODEL // (3 * 256)) * 256

LANES = 128
MASK_VALUE = -1e30

BF16 = jnp.bfloat16
F32 = jnp.float32

VMEM_LIMIT = 56 * 1024 * 1024


def _params(n_axes):
    return pltpu.CompilerParams(dimension_semantics=("arbitrary",) * n_axes,
                                vmem_limit_bytes=VMEM_LIMIT)


def _silu(x):
    return x / (1.0 + jnp.exp(-x))


NORM_ROWS = 128


def _fill_normed(x_ref, g_ref, xn_ref):
    gain = g_ref[...]

    def body(i, carry):
        rows = pl.ds(pl.multiple_of(i * NORM_ROWS, NORM_ROWS), NORM_ROWS)
        xv = x_ref[rows, :]
        ms = jnp.mean(xv * xv, axis=-1, keepdims=True)
        xn_ref[rows, :] = ((xv * lax.rsqrt(ms + EPS)) * gain).astype(BF16)
        return carry

    lax.fori_loop(0, x_ref.shape[0] // NORM_ROWS, body, 0)


def _group_norm_store(y, gn_ref, o_ref, scale):
    gain = gn_ref[...]
    for g in range(y.shape[1] // LANES):
        yg = y[:, g * LANES:(g + 1) * LANES]
        ms = jnp.mean(yg * yg, axis=-1, keepdims=True)
        val = (yg * lax.rsqrt(ms + EPS)) * gain
        if scale != 1.0:
            val = val * scale
        o_ref[:, g * LANES:(g + 1) * LANES] = val.astype(o_ref.dtype)


NT_DIMS = (((1,), (1,)), ((), ()))


def _norm_proj_kernel(x_ref, g_ref, w_ref, *rest, group_norm, scale, gate_rank, w_transposed):
    rest = list(rest)
    w1_ref = rest.pop(0) if gate_rank else None
    gn_ref = rest.pop(0) if group_norm else None
    o_ref = rest.pop(0)
    a_ref = rest.pop(0) if gate_rank else None
    xn_ref = rest.pop(0)

    def project(w_blk):
        if w_transposed:
            return lax.dot_general(xn_ref[...], w_blk, NT_DIMS, preferred_element_type=F32)
        return jnp.dot(xn_ref[...], w_blk, preferred_element_type=F32)

    @pl.when(pl.program_id(1) == 0)
    def _():
        _fill_normed(x_ref, g_ref, xn_ref)
        if gate_rank:
            a_ref[:, :gate_rank] = project(w1_ref[...])
            a_ref[:, gate_rank:] = jnp.zeros((a_ref.shape[0], a_ref.shape[1] - gate_rank), F32)

    y = project(w_ref[...])
    if group_norm:
        _group_norm_store(y, gn_ref, o_ref, scale)
    else:
        o_ref[...] = y.astype(o_ref.dtype)


def _w_spec(k, bn, layer, col_block0):
    return pl.BlockSpec((None, k, bn), lambda i, j: (layer, 0, j + col_block0))


def _norm_proj(x, gain, w, layer, *, n, bm, bn, out_dtype, col0=0, gate_rank=0, gn_gain=None, scale=1.0,
               w_transposed=False):
    m, k = x.shape
    group_norm = gn_gain is not None
    assert not gate_rank or w_transposed
    if w_transposed:
        w_spec = pl.BlockSpec((None, bn, k), lambda i, j: (layer, j + col0 // bn, 0))
    else:
        w_spec = _w_spec(k, bn, layer, col0 // bn)
    in_specs = [pl.BlockSpec((bm, k), lambda i, j: (i, 0)),
                pl.BlockSpec((1, k), lambda i, j: (0, 0)),
                w_spec]
    args = [x, gain.reshape(1, k), w]
    if gate_rank:
        in_specs.append(pl.BlockSpec((None, gate_rank, k), lambda i, j: (layer, (col0 + n) // gate_rank, 0)))
        args.append(w)
    if group_norm:
        in_specs.append(pl.BlockSpec((1, LANES), lambda i, j: (0, 0)))
        args.append(gn_gain.reshape(1, LANES))
    out_shape = [jax.ShapeDtypeStruct((m, n), out_dtype)]
    out_specs = [pl.BlockSpec((bm, bn), lambda i, j: (i, j))]
    if gate_rank:
        out_shape.append(jax.ShapeDtypeStruct((m, LANES), F32))
        out_specs.append(pl.BlockSpec((bm, LANES), lambda i, j: (i, 0)))
    res = pl.pallas_call(
        functools.partial(_norm_proj_kernel, group_norm=group_norm, scale=scale, gate_rank=gate_rank,
                          w_transposed=w_transposed),
        grid=(m // bm, n // bn),
        in_specs=in_specs,
        out_specs=out_specs,
        out_shape=out_shape,
        scratch_shapes=[pltpu.VMEM((bm, k), BF16)],
        compiler_params=_params(2),
        name="norm_proj",
    )(*args)
    return res if gate_rank else res[0]


CAST_ROWS = 256


def _cast_tile(src_ref, dst_ref):
    def body(i, carry):
        rows = pl.ds(pl.multiple_of(i * CAST_ROWS, CAST_ROWS), CAST_ROWS)
        dst_ref[rows, :] = src_ref[rows, :].astype(dst_ref.dtype)
        return carry

    lax.fori_loop(0, src_ref.shape[0] // CAST_ROWS, body, 0)


def _ffn_up_kernel(xn_ref, wg_ref, wu_ref, o_ref, wgb_ref, wub_ref):
    @pl.when(pl.program_id(1) == 0)
    def _():
        _cast_tile(wg_ref, wgb_ref)
        _cast_tile(wu_ref, wub_ref)

    xn = xn_ref[...]
    g = jnp.dot(xn, wgb_ref[...], preferred_element_type=F32)
    u = jnp.dot(xn, wub_ref[...], preferred_element_type=F32)
    o_ref[...] = (_silu(g) * u).astype(o_ref.dtype)


def _ffn_up(xn, w_in, layer, *, bm, bn):
    m, k = xn.shape
    nblk = D_FF // bn
    return pl.pallas_call(
        _ffn_up_kernel,
        grid=(nblk, m // bm),
        in_specs=[pl.BlockSpec((bm, k), lambda j, i: (i, 0)),
                  pl.BlockSpec((None, k, bn), lambda j, i: (layer, 0, j)),
                  pl.BlockSpec((None, k, bn), lambda j, i: (layer, 0, j + nblk))],
        out_specs=pl.BlockSpec((bm, bn), lambda j, i: (i, j)),
        out_shape=jax.ShapeDtypeStruct((m, D_FF), BF16),
        scratch_shapes=[pltpu.VMEM((k, bn), BF16), pltpu.VMEM((k, bn), BF16)],
        compiler_params=_params(2),
        name="ffn_up",
    )(xn, w_in, w_in)


def _res_mm_norm_kernel(a_ref, w_ref, r_ref, g_ref, o_ref, xn_ref):
    o_ref[...] = r_ref[...] + jnp.dot(a_ref[...], w_ref[...], preferred_element_type=F32)
    _fill_normed(o_ref, g_ref, xn_ref)


def _res_mm_norm(a, w, layer, res, gain, *, bm):
    m, k = a.shape
    n = w.shape[2]
    row_blk = lambda i: (i, 0)
    return pl.pallas_call(
        _res_mm_norm_kernel,
        grid=(m // bm,),
        in_specs=[pl.BlockSpec((bm, k), row_blk),
                  pl.BlockSpec((None, k, n), lambda i: (layer, 0, 0)),
                  pl.BlockSpec((bm, n), row_blk),
                  pl.BlockSpec((1, n), lambda i: (0, 0))],
        out_specs=[pl.BlockSpec((bm, n), row_blk), pl.BlockSpec((bm, n), row_blk)],
        out_shape=[jax.ShapeDtypeStruct((m, n), F32), jax.ShapeDtypeStruct((m, n), BF16)],
        compiler_params=_params(1),
        name="res_mm_norm",
    )(a, w, res, gain.reshape(1, n))


def _res_mm_kernel(a_ref, w_ref, r_ref, o_ref):
    o_ref[...] = r_ref[...] + jnp.dot(a_ref[...], w_ref[...], preferred_element_type=F32)


def _res_mm(a, w, layer, res, *, bm, bn):
    m, k = a.shape
    n = w.shape[2]
    return pl.pallas_call(
        _res_mm_kernel,
        grid=(m // bm, n // bn),
        in_specs=[pl.BlockSpec((bm, k), lambda i, j: (i, 0)),
                  _w_spec(k, bn, layer, 0),
                  pl.BlockSpec((bm, bn), lambda i, j: (i, j))],
        out_specs=pl.BlockSpec((bm, bn), lambda i, j: (i, j)),
        out_shape=jax.ShapeDtypeStruct((m, n), F32),
        compiler_params=_params(2),
        name="res_mm",
    )(a, w, res)


GLA_ROWS = 512


def _split2(x):
    hi = x.astype(BF16)
    lo = (x - hi.astype(F32)).astype(BF16)
    return hi, lo


def _gla_kernel(p_ref, a_ref, w2_ref, b_ref, hg_ref, o_ref, state_ref):
    c = GLA_CHUNK

    @pl.when(pl.program_id(1) == 0)
    def _():
        state_ref[...] = jnp.zeros_like(state_ref)

    row = lax.broadcasted_iota(jnp.int32, (c, c), 0)
    col = lax.broadcasted_iota(jnp.int32, (c, c), 1)
    causal = row >= col
    tril = causal.astype(BF16)
    head_gain = hg_ref[...]
    tn_dims = (((0,), (0,)), ((), ()))
    nt_dims = (((1,), (1,)), ((), ()))

    heads = range(GLA_HEADS)
    hk = [slice(h * GLA_HK, (h + 1) * GLA_HK) for h in heads]
    hv = [slice(h * GLA_HV, (h + 1) * GLA_HV) for h in heads]

    def rows_of(ci):
        return slice(ci * c, (ci + 1) * c)

    def gate_preact(ci):
        a_low = a_ref[rows_of(ci), :].astype(BF16)
        return jnp.dot(a_low, w2_ref[...], preferred_element_type=F32) + b_ref[...]

    def cum_log_decay(z):
        log_a = (jnp.minimum(z, 0.0) - jnp.log(1.0 + jnp.exp(-jnp.abs(z)))) / GLA_GATE_TAU
        return sum(jnp.dot(tril, part, preferred_element_type=F32) for part in _split2(log_a))

    def decayed_operands(ci, bcum):
        b_last = bcum[c - 1:c, :]
        q = p_ref[rows_of(ci), 0:GLA_DK]
        k = p_ref[rows_of(ci), GLA_DK:2 * GLA_DK]
        q_dec = ((q * (GLA_HK ** -0.5)) * jnp.exp(bcum)).astype(BF16)
        k_inv = (k * jnp.exp(-bcum)).astype(BF16)
        k_end = (k * jnp.exp(b_last - bcum)).astype(BF16)
        return q_dec, k_inv, k_end, jnp.exp(b_last)

    n_chunks = GLA_ROWS // c
    prepared = decayed_operands(0, cum_log_decay(gate_preact(0)))
    for ci in range(n_chunks):
        rows = rows_of(ci)
        q_dec, k_inv, k_end, chunk_decay = prepared
        has_next = ci + 1 < n_chunks
        if has_next:
            z_next = gate_preact(ci + 1)
        v_b = p_ref[rows, 2 * GLA_DK:2 * GLA_DK + GLA_DV].astype(BF16)
        states = [state_ref[h] for h in heads]
        o_inter = [jnp.dot(q_dec[:, hk[h]], states[h].astype(BF16), preferred_element_type=F32) for h in heads]
        kv = [lax.dot_general(k_end[:, hk[h]], v_b[:, hv[h]], tn_dims, preferred_element_type=F32)
              for h in heads]
        attn = [lax.dot_general(q_dec[:, hk[h]], k_inv[:, hk[h]], nt_dims, preferred_element_type=F32)
                for h in heads]
        if has_next:
            bcum_next = cum_log_decay(z_next)
        o_intra = [jnp.dot(jnp.where(causal, attn[h], 0.0).astype(BF16), v_b[:, hv[h]],
                           preferred_element_type=F32) for h in heads]
        if has_next:
            prepared = decayed_operands(ci + 1, bcum_next)
        for h in heads:
            decay = jnp.broadcast_to(chunk_decay[:, hk[h]], (LANES, GLA_HK)).T
            state_ref[h] = states[h] * jnp.concatenate([decay] * (GLA_HV // LANES), axis=1) + kv[h]
            o = o_intra[h] + o_inter[h]
            ms = jnp.mean(o * o, axis=-1, keepdims=True)
            o = (o * lax.rsqrt(ms + EPS)) * head_gain
            r = p_ref[rows, 2 * GLA_DK + GLA_DV + h * GLA_HV:2 * GLA_DK + GLA_DV + (h + 1) * GLA_HV]
            o_ref[rows, hv[h]] = (o * _silu(r)).astype(o_ref.dtype)


def _gla_core(proj, a_low, w2, b_alpha, head_gain):
    nblk = SEQ // GLA_ROWS
    return pl.pallas_call(
        _gla_kernel,
        grid=(BATCH, nblk),
        in_specs=[pl.BlockSpec((GLA_ROWS, GLA_MAIN), lambda b, n: (b * nblk + n, 0)),
                  pl.BlockSpec((GLA_ROWS, LANES), lambda b, n: (b * nblk + n, 0)),
                  pl.BlockSpec((LANES, GLA_DK), lambda b, n: (0, 0)),
                  pl.BlockSpec((1, GLA_DK), lambda b, n: (0, 0)),
                  pl.BlockSpec((1, GLA_HV), lambda b, n: (0, 0))],
        out_specs=pl.BlockSpec((GLA_ROWS, GLA_DV), lambda b, n: (b * nblk + n, 0)),
        out_shape=jax.ShapeDtypeStruct((TOKENS, GLA_DV), BF16),
        scratch_shapes=[pltpu.VMEM((GLA_HEADS, GLA_HK, GLA_HV), F32)],
        compiler_params=_params(2),
        name="gla_core",
    )(proj, a_low, w2, b_alpha.reshape(1, GLA_DK), head_gain.reshape(1, GLA_HV))


ATT_TQ = 256
LOG2E = math.log2(math.e)


def _lane_tiles(x):
    return [x[:, i * LANES:(i + 1) * LANES] for i in range(x.shape[1] // LANES)]


def _attn_kernel(slope_ref, q_ref, k_ref, v_ref, lq1_ref, lk1_ref, lq2_ref, lk2_ref, hg_ref, o_ref,
                 bias_ref, *, lambda_init):
    tq, dh = ATT_TQ, DIFF_HEAD_DIM
    nt_dims = (((1,), (1,)), ((), ()))
    slope2 = slope_ref[0][:, :1] * LOG2E
    row = lax.broadcasted_iota(jnp.int32, (tq, SEQ), 0)
    col = lax.broadcasted_iota(jnp.int32, (tq, SEQ), 1)
    dist = row - (col - (SEQ - tq))
    bias_ref[...] = jnp.where(dist >= 0, -slope2 * dist.astype(F32), MASK_VALUE)
    lam = (jnp.exp(jnp.sum(lq1_ref[...] * lk1_ref[...], axis=-1, keepdims=True))
           - jnp.exp(jnp.sum(lq2_ref[...] * lk2_ref[...], axis=-1, keepdims=True))
           + lambda_init)
    head_gain = hg_ref[...]

    def scores(qi):
        q0, n_keys = qi * tq, (qi + 1) * tq
        return [lax.dot_general(q_ref[q0:q0 + tq, comp * dh:(comp + 1) * dh],
                                k_ref[0:n_keys, comp * dh:(comp + 1) * dh],
                                nt_dims, preferred_element_type=F32) for comp in range(2)]

    def attend(qi, qk):
        q0, n_keys = qi * tq, (qi + 1) * tq
        v_blk = v_ref[0:n_keys, :]
        outs = []
        for comp in range(2):
            tiles = _lane_tiles(qk[comp] + bias_ref[:, SEQ - n_keys:])
            m = jnp.max(functools.reduce(jnp.maximum, tiles), axis=-1, keepdims=True)
            p_tiles = [jnp.exp2(ti - m) for ti in tiles]
            l = jnp.sum(functools.reduce(jnp.add, p_tiles), axis=-1, keepdims=True)
            p = jnp.concatenate([pt.astype(BF16) for pt in p_tiles], axis=1)
            outs.append(jnp.dot(p, v_blk, preferred_element_type=F32) / l)
        o = outs[0] - lam * outs[1]
        ms = jnp.mean(o * o, axis=-1, keepdims=True)
        o = ((o * lax.rsqrt(ms + EPS)) * head_gain) * (1.0 - lambda_init)
        o_ref[q0:q0 + tq, :] = o.astype(o_ref.dtype)

    order = list(reversed(range(SEQ // tq)))
    qk_next = scores(order[0])
    for n, qi in enumerate(order):
        qk = qk_next
        if n + 1 < len(order):
            qk_next = scores(order[n + 1])
        attend(qi, qk)


def _diff_attn_core(q, k, v, slopes, lq1, lk1, lq2, lk2, head_gain, lambda_init):
    blk = pl.BlockSpec((SEQ, DIFF_VDIM), lambda b, h: (b, h))
    vec = pl.BlockSpec((1, DIFF_HEAD_DIM), lambda b, h: (0, 0))
    return pl.pallas_call(
        functools.partial(_attn_kernel, lambda_init=lambda_init),
        grid=(BATCH, DIFF_HEADS),
        in_specs=[pl.BlockSpec((1, 1, LANES), lambda b, h: (h, 0, 0)),
                  blk, blk, blk, vec, vec, vec, vec,
                  pl.BlockSpec((1, DIFF_VDIM), lambda b, h: (0, 0))],
        out_specs=blk,
        out_shape=jax.ShapeDtypeStruct((TOKENS, DIFF_V), BF16),
        scratch_shapes=[pltpu.VMEM((ATT_TQ, SEQ), F32)],
        compiler_params=_params(2),
        name="diff_attn",
    )(slopes, q, k, v,
      lq1.reshape(1, DIFF_HEAD_DIM), lk1.reshape(1, DIFF_HEAD_DIM),
      lq2.reshape(1, DIFF_HEAD_DIM), lk2.reshape(1, DIFF_HEAD_DIM),
      head_gain.reshape(1, DIFF_VDIM))


CAST_BLOCK_BYTES = 8 * 1024 * 1024


def _cast_kernel(x_ref, o_ref):
    o_ref[...] = x_ref[...].astype(o_ref.dtype)


def _to_bf16(w):
    l, r, c = w.shape
    bc = next((t for t in (2048, 1024) if c % t == 0), c)
    br = max(t for t in range(16, r + 1, 16) if r % t == 0 and t * bc * 4 <= CAST_BLOCK_BYTES)
    spec = pl.BlockSpec((None, br, bc), lambda n, i, j: (n, i, j))
    return pl.pallas_call(
        _cast_kernel,
        grid=(l, r // br, c // bc),
        in_specs=[spec],
        out_specs=spec,
        out_shape=jax.ShapeDtypeStruct((l, r, c), BF16),
        compiler_params=_params(3),
        name="cast_bf16",
    )(w)


def kernel(x, gla_attn_norm, gla_w_in, gla_w_alpha2, gla_b_alpha, gla_head_norm, gla_w_out, kv_norm, w_kv, k_norm, diff_attn_norm, diff_w_q, diff_q_norm, diff_lambda_q1, diff_lambda_k1, diff_lambda_q2, diff_lambda_k2, diff_head_norm, diff_w_out, ffn_norm, ffn_w_in, ffn_w_out):
    xt = x.reshape(TOKENS, D_MODEL)
    pad = LANES - GLA_GATE_RANK
    slopes = (2.0 ** (-8.0 * jnp.arange(1, DIFF_HEADS + 1, dtype=F32) / DIFF_HEADS))
    slopes = jnp.broadcast_to(slopes[:, None, None], (DIFF_HEADS, 1, LANES))

    gla_in_b = _to_bf16(jnp.swapaxes(gla_w_in, 1, 2))
    gla_out_b = _to_bf16(gla_w_out)
    kv_b = _to_bf16(w_kv[None])
    wq_b = _to_bf16(diff_w_q)
    wo_b = _to_bf16(diff_w_out)
    ffn_out_b = _to_bf16(ffn_w_out)

    k_shared = v_shared = None
    for l in range(DEPTH):
        if l < N_A_LAYERS:
            i = l
            w2 = jnp.pad(gla_w_alpha2[i], ((0, pad), (0, 0))).astype(BF16)
            proj, a_low = _norm_proj(xt, gla_attn_norm[i], gla_in_b, i, n=GLA_MAIN, bm=1024, bn=1024,
                                     out_dtype=F32, gate_rank=GLA_GATE_RANK, w_transposed=True)
            o = _gla_core(proj, a_low, w2, gla_b_alpha[i], gla_head_norm[i])
            xt, xn = _res_mm_norm(o, gla_out_b, i, xt, ffn_norm[l], bm=512)
        else:
            if l == N_A_LAYERS:
                k_shared = _norm_proj(xt, kv_norm, kv_b, 0, n=DIFF_QK, bm=1024, bn=1024,
                                      out_dtype=BF16, gn_gain=k_norm)
                v_shared = _norm_proj(xt, kv_norm, kv_b, 0, n=DIFF_V, col0=DIFF_QK, bm=1024, bn=1024,
                                      out_dtype=BF16)
            j = l - N_A_LAYERS
            lambda_init = 0.8 - 0.6 * math.exp(-0.3 * l)
            q = _norm_proj(xt, diff_attn_norm[j], wq_b, j, n=DIFF_QK, bm=1024, bn=1024,
                           out_dtype=BF16, gn_gain=diff_q_norm[j], scale=DIFF_HEAD_DIM ** -0.5 * LOG2E)
            o = _diff_attn_core(q, k_shared, v_shared, slopes, diff_lambda_q1[j], diff_lambda_k1[j],
                                diff_lambda_q2[j], diff_lambda_k2[j], diff_head_norm[j], lambda_init)
            xt, xn = _res_mm_norm(o, wo_b, j, xt, ffn_norm[l], bm=512)
        act = _ffn_up(xn, ffn_w_in, l, bm=1024, bn=512)
        xt = _res_mm(act, ffn_out_b, l, xt, bm=1024, bn=512)
    return xt.reshape(BATCH, SEQ, D_MODEL)
```

```python
import functools
import math

import jax
import jax.numpy as jnp
from jax import lax
from jax.experimental import pallas as pl
from jax.experimental.pallas import tpu as pltpu

D_MODEL = 2048
BATCH = 4
SEQ = 2048
DEPTH = 4
TOKENS = BATCH * SEQ
N_A_LAYERS = DEPTH // 2
N_B_LAYERS = DEPTH - N_A_LAYERS
EPS = 1e-6

GLA_HEADS = 4
GLA_DK = D_MODEL // 2
GLA_DV = D_MODEL
GLA_HK = GLA_DK // GLA_HEADS
GLA_HV = GLA_DV // GLA_HEADS
GLA_GATE_RANK = 16
GLA_GATE_TAU = 16.0
GLA_CHUNK = 64
GLA_MAIN = 2 * GLA_DK + 2 * GLA_DV

DIFF_HEAD_DIM = 128
DIFF_HEADS = D_MODEL // (2 * DIFF_HEAD_DIM)
DIFF_VDIM = 2 * DIFF_HEAD_DIM
DIFF_QK = DIFF_HEADS * 2 * DIFF_HEAD_DIM
DIFF_V = DIFF_HEADS * DIFF_VDIM

D_FF = -(-8 * D_MODEL // (3 * 256)) * 256

LANES = 128
MASK_VALUE = -1e30

BF16 = jnp.bfloat16
F32 = jnp.float32

VMEM_LIMIT = 56 * 1024 * 1024


def _params(n_axes):
    return pltpu.CompilerParams(dimension_semantics=("arbitrary",) * n_axes,
                                vmem_limit_bytes=VMEM_LIMIT)


def _silu(x):
    return x / (1.0 + jnp.exp(-x))


NORM_ROWS = 128


def _fill_normed(x_ref, g_ref, xn_ref):
    gain = g_ref[...]

    def body(i, carry):
        rows = pl.ds(pl.multiple_of(i * NORM_ROWS, NORM_ROWS), NORM_ROWS)
        xv = x_ref[rows, :]
        ms = jnp.mean(xv * xv, axis=-1, keepdims=True)
        xn_ref[rows, :] = ((xv * lax.rsqrt(ms + EPS)) * gain).astype(BF16)
        return carry

    lax.fori_loop(0, x_ref.shape[0] // NORM_ROWS, body, 0)


def _group_norm_store(y, gn_ref, o_ref, scale):
    gain = gn_ref[...]
    for g in range(y.shape[1] // LANES):
        yg = y[:, g * LANES:(g + 1) * LANES]
        ms = jnp.mean(yg * yg, axis=-1, keepdims=True)
        val = (yg * lax.rsqrt(ms + EPS)) * gain
        if scale != 1.0:
            val = val * scale
        o_ref[:, g * LANES:(g + 1) * LANES] = val.astype(o_ref.dtype)


NT_DIMS = (((1,), (1,)), ((), ()))


def _norm_proj_kernel(x_ref, g_ref, w_ref, *rest, group_norm, scale, gate_rank, w_transposed):
    rest = list(rest)
    w1_ref = rest.pop(0) if gate_rank else None
    gn_ref = rest.pop(0) if group_norm else None
    o_ref = rest.pop(0)
    a_ref = rest.pop(0) if gate_rank else None
    xn_ref = rest.pop(0)

    def project(w_blk):
        if w_transposed:
            return lax.dot_general(xn_ref[...], w_blk, NT_DIMS, preferred_element_type=F32)
        return jnp.dot(xn_ref[...], w_blk, preferred_element_type=F32)

    @pl.when(pl.program_id(1) == 0)
    def _():
        _fill_normed(x_ref, g_ref, xn_ref)
        if gate_rank:
            a_ref[:, :gate_rank] = project(w1_ref[...])
            a_ref[:, gate_rank:] = jnp.zeros((a_ref.shape[0], a_ref.shape[1] - gate_rank), F32)

    y = project(w_ref[...])
    if group_norm:
        _group_norm_store(y, gn_ref, o_ref, scale)
    else:
        o_ref[...] = y.astype(o_ref.dtype)


def _w_spec(k, bn, layer, col_block0):
    return pl.BlockSpec((None, k, bn), lambda i, j: (layer, 0, j + col_block0))


def _norm_proj(x, gain, w, layer, *, n, bm, bn, out_dtype, col0=0, gate_rank=0, gn_gain=None, scale=1.0,
               w_transposed=False):
    m, k = x.shape
    group_norm = gn_gain is not None
    assert not gate_rank or w_transposed
    if w_transposed:
        w_spec = pl.BlockSpec((None, bn, k), lambda i, j: (layer, j + col0 // bn, 0))
    else:
        w_spec = _w_spec(k, bn, layer, col0 // bn)
    in_specs = [pl.BlockSpec((bm, k), lambda i, j: (i, 0)),
                pl.BlockSpec((1, k), lambda i, j: (0, 0)),
                w_spec]
    args = [x, gain.reshape(1, k), w]
    if gate_rank:
        in_specs.append(pl.BlockSpec((None, gate_rank, k), lambda i, j: (layer, (col0 + n) // gate_rank, 0)))
        args.append(w)
    if group_norm:
        in_specs.append(pl.BlockSpec((1, LANES), lambda i, j: (0, 0)))
        args.append(gn_gain.reshape(1, LANES))
    out_shape = [jax.ShapeDtypeStruct((m, n), out_dtype)]
    out_specs = [pl.BlockSpec((bm, bn), lambda i, j: (i, j))]
    if gate_rank:
        out_shape.append(jax.ShapeDtypeStruct((m, LANES), F32))
        out_specs.append(pl.BlockSpec((bm, LANES), lambda i, j: (i, 0)))
    res = pl.pallas_call(
        functools.partial(_norm_proj_kernel, group_norm=group_norm, scale=scale, gate_rank=gate_rank,
                          w_transposed=w_transposed),
        grid=(m // bm, n // bn),
        in_specs=in_specs,
        out_specs=out_specs,
        out_shape=out_shape,
        scratch_shapes=[pltpu.VMEM((bm, k), BF16)],
        compiler_params=_params(2),
        name="norm_proj",
    )(*args)
    return res if gate_rank else res[0]


CAST_ROWS = 256


def _cast_tile(src_ref, dst_ref):
    def body(i, carry):
        rows = pl.ds(pl.multiple_of(i * CAST_ROWS, CAST_ROWS), CAST_ROWS)
        dst_ref[rows, :] = src_ref[rows, :].astype(dst_ref.dtype)
        return carry

    lax.fori_loop(0, src_ref.shape[0] // CAST_ROWS, body, 0)


def _ffn_up_kernel(xn_ref, wg_ref, wu_ref, *rest, n_side):
    side_src = rest[:n_side]
    o_ref = rest[n_side]
    side_dst = rest[n_side + 1:2 * n_side + 1]
    wgb_ref, wub_ref = rest[2 * n_side + 1:]

    @pl.when(pl.program_id(1) == 0)
    def _():
        _cast_tile(wg_ref, wgb_ref)
        _cast_tile(wu_ref, wub_ref)

    xn = xn_ref[...]
    half = o_ref.shape[1] // 2
    for cols in (slice(0, half), slice(half, 2 * half)):
        g = jnp.dot(xn, wgb_ref[:, cols], preferred_element_type=F32)
        u = jnp.dot(xn, wub_ref[:, cols], preferred_element_type=F32)
        o_ref[:, cols] = (_silu(g) * u).astype(o_ref.dtype)

    for src, dst in zip(side_src, side_dst):
        dst[...] = src[...].astype(dst.dtype)


def _ffn_up(xn, w_in, layer, side, *, bm, bn):
    m, k = xn.shape
    nblk, mblk = D_FF // bn, m // bm
    n_steps = nblk * mblk
    side_in, side_out, side_shapes = [], [], []
    for w, w_layer in side:
        _, r, c = w.shape
        rows = min(t for t in range(16, r + 1, 16) if r % t == 0 and r // t <= n_steps)
        last = r // rows - 1
        side_in.append(pl.BlockSpec(
            (None, rows, c), lambda j, i, w_layer=w_layer, last=last: (w_layer, jnp.minimum(j * mblk + i, last), 0)))
        side_out.append(pl.BlockSpec((rows, c), lambda j, i, last=last: (jnp.minimum(j * mblk + i, last), 0)))
        side_shapes.append(jax.ShapeDtypeStruct((r, c), BF16))
    res = pl.pallas_call(
        functools.partial(_ffn_up_kernel, n_side=len(side)),
        grid=(nblk, mblk),
        in_specs=[pl.BlockSpec((bm, k), lambda j, i: (i, 0)),
                  pl.BlockSpec((None, k, bn), lambda j, i: (layer, 0, j)),
                  pl.BlockSpec((None, k, bn), lambda j, i: (layer, 0, j + nblk))] + side_in,
        out_specs=[pl.BlockSpec((bm, bn), lambda j, i: (i, j))] + side_out,
        out_shape=[jax.ShapeDtypeStruct((m, D_FF), BF16)] + side_shapes,
        scratch_shapes=[pltpu.VMEM((k, bn), BF16), pltpu.VMEM((k, bn), BF16)],
        compiler_params=_params(2),
        name="ffn_up",
    )(xn, w_in, w_in, *[w for w, _ in side])
    return res[0], res[1:]


def _res_mm_norm_kernel(a_ref, w_ref, r_ref, g_ref, o_ref, xn_ref):
    o_ref[...] = r_ref[...] + jnp.dot(a_ref[...], w_ref[...], preferred_element_type=F32)
    _fill_normed(o_ref, g_ref, xn_ref)


def _res_mm_norm(a, w, layer, res, gain, *, bm):
    m, k = a.shape
    n = w.shape[2]
    row_blk = lambda i: (i, 0)
    return pl.pallas_call(
        _res_mm_norm_kernel,
        grid=(m // bm,),
        in_specs=[pl.BlockSpec((bm, k), row_blk),
                  pl.BlockSpec((None, k, n), lambda i: (layer, 0, 0)),
                  pl.BlockSpec((bm, n), row_blk),
                  pl.BlockSpec((1, n), lambda i: (0, 0))],
        out_specs=[pl.BlockSpec((bm, n), row_blk), pl.BlockSpec((bm, n), row_blk)],
        out_shape=[jax.ShapeDtypeStruct((m, n), F32), jax.ShapeDtypeStruct((m, n), BF16)],
        compiler_params=_params(1),
        name="res_mm_norm",
    )(a, w, res, gain.reshape(1, n))


def _res_mm_kernel(a_ref, w_ref, r_ref, o_ref):
    o_ref[...] = r_ref[...] + jnp.dot(a_ref[...], w_ref[...], preferred_element_type=F32)


def _res_mm(a, w, layer, res, *, bm, bn):
    m, k = a.shape
    n = w.shape[2]
    return pl.pallas_call(
        _res_mm_kernel,
        grid=(m // bm, n // bn),
        in_specs=[pl.BlockSpec((bm, k), lambda i, j: (i, 0)),
                  _w_spec(k, bn, layer, 0),
                  pl.BlockSpec((bm, bn), lambda i, j: (i, j))],
        out_specs=pl.BlockSpec((bm, bn), lambda i, j: (i, j)),
        out_shape=jax.ShapeDtypeStruct((m, n), F32),
        compiler_params=_params(2),
        name="res_mm",
    )(a, w, res)


GLA_ROWS = 512


def _split2(x):
    hi = x.astype(BF16)
    lo = (x - hi.astype(F32)).astype(BF16)
    return hi, lo


def _gla_kernel(p_ref, a_ref, w2_ref, b_ref, hg_ref, o_ref, state_ref):
    c = GLA_CHUNK

    @pl.when(pl.program_id(1) == 0)
    def _():
        state_ref[...] = jnp.zeros_like(state_ref)

    row = lax.broadcasted_iota(jnp.int32, (c, c), 0)
    col = lax.broadcasted_iota(jnp.int32, (c, c), 1)
    causal = row >= col
    tril = causal.astype(BF16)
    head_gain = hg_ref[...]
    tn_dims = (((0,), (0,)), ((), ()))
    nt_dims = (((1,), (1,)), ((), ()))

    heads = range(GLA_HEADS)
    hk = [slice(h * GLA_HK, (h + 1) * GLA_HK) for h in heads]
    hv = [slice(h * GLA_HV, (h + 1) * GLA_HV) for h in heads]

    def rows_of(ci):
        return slice(ci * c, (ci + 1) * c)

    def gate_preact(ci):
        a_low = a_ref[rows_of(ci), :].astype(BF16)
        return jnp.dot(a_low, w2_ref[...], preferred_element_type=F32) + b_ref[...]

    def cum_log_decay(z):
        log_a = (jnp.minimum(z, 0.0) - jnp.log(1.0 + jnp.exp(-jnp.abs(z)))) / GLA_GATE_TAU
        return sum(jnp.dot(tril, part, preferred_element_type=F32) for part in _split2(log_a))

    def decayed_operands(ci, bcum):
        b_last = bcum[c - 1:c, :]
        q = p_ref[rows_of(ci), 0:GLA_DK]
        k = p_ref[rows_of(ci), GLA_DK:2 * GLA_DK]
        q_dec = ((q * (GLA_HK ** -0.5)) * jnp.exp(bcum)).astype(BF16)
        k_inv = (k * jnp.exp(-bcum)).astype(BF16)
        k_end = (k * jnp.exp(b_last - bcum)).astype(BF16)
        return q_dec, k_inv, k_end, jnp.exp(b_last)

    n_chunks = GLA_ROWS // c
    prepared = decayed_operands(0, cum_log_decay(gate_preact(0)))
    for ci in range(n_chunks):
        rows = rows_of(ci)
        q_dec, k_inv, k_end, chunk_decay = prepared
        has_next = ci + 1 < n_chunks
        if has_next:
            z_next = gate_preact(ci + 1)
        v_b = p_ref[rows, 2 * GLA_DK:2 * GLA_DK + GLA_DV].astype(BF16)
        states = [state_ref[h] for h in heads]
        o_inter = [jnp.dot(q_dec[:, hk[h]], states[h].astype(BF16), preferred_element_type=F32) for h in heads]
        kv = [lax.dot_general(k_end[:, hk[h]], v_b[:, hv[h]], tn_dims, preferred_element_type=F32)
              for h in heads]
        attn = [lax.dot_general(q_dec[:, hk[h]], k_inv[:, hk[h]], nt_dims, preferred_element_type=F32)
                for h in heads]
        if has_next:
            bcum_next = cum_log_decay(z_next)
        o_intra = [jnp.dot(jnp.where(causal, attn[h], 0.0).astype(BF16), v_b[:, hv[h]],
                           preferred_element_type=F32) for h in heads]
        if has_next:
            prepared = decayed_operands(ci + 1, bcum_next)
        for h in heads:
            decay = jnp.broadcast_to(chunk_decay[:, hk[h]], (LANES, GLA_HK)).T
            state_ref[h] = states[h] * jnp.concatenate([decay] * (GLA_HV // LANES), axis=1) + kv[h]
            o = o_intra[h] + o_inter[h]
            ms = jnp.mean(o * o, axis=-1, keepdims=True)
            o = (o * lax.rsqrt(ms + EPS)) * head_gain
            r = p_ref[rows, 2 * GLA_DK + GLA_DV + h * GLA_HV:2 * GLA_DK + GLA_DV + (h + 1) * GLA_HV]
            o_ref[rows, hv[h]] = (o * _silu(r)).astype(o_ref.dtype)


def _gla_core(proj, a_low, w2, b_alpha, head_gain):
    nblk = SEQ // GLA_ROWS
    return pl.pallas_call(
        _gla_kernel,
        grid=(BATCH, nblk),
        in_specs=[pl.BlockSpec((GLA_ROWS, GLA_MAIN), lambda b, n: (b * nblk + n, 0)),
                  pl.BlockSpec((GLA_ROWS, LANES), lambda b, n: (b * nblk + n, 0)),
                  pl.BlockSpec((LANES, GLA_DK), lambda b, n: (0, 0)),
                  pl.BlockSpec((1, GLA_DK), lambda b, n: (0, 0)),
                  pl.BlockSpec((1, GLA_HV), lambda b, n: (0, 0))],
        out_specs=pl.BlockSpec((GLA_ROWS, GLA_DV), lambda b, n: (b * nblk + n, 0)),
        out_shape=jax.ShapeDtypeStruct((TOKENS, GLA_DV), BF16),
        scratch_shapes=[pltpu.VMEM((GLA_HEADS, GLA_HK, GLA_HV), F32)],
        compiler_params=_params(2),
        name="gla_core",
    )(proj, a_low, w2, b_alpha.reshape(1, GLA_DK), head_gain.reshape(1, GLA_HV))


ATT_TQ = 256
LOG2E = math.log2(math.e)


def _lane_tiles(x):
    return [x[:, i * LANES:(i + 1) * LANES] for i in range(x.shape[1] // LANES)]


def _attn_kernel(slope_ref, q_ref, k_ref, v_ref, lq1_ref, lk1_ref, lq2_ref, lk2_ref, hg_ref, o_ref,
                 bias_ref, *, lambda_init):
    tq, dh = ATT_TQ, DIFF_HEAD_DIM
    nt_dims = (((1,), (1,)), ((), ()))
    slope2 = slope_ref[0][:, :1] * LOG2E
    row = lax.broadcasted_iota(jnp.int32, (tq, SEQ), 0)
    col = lax.broadcasted_iota(jnp.int32, (tq, SEQ), 1)
    dist = row - (col - (SEQ - tq))
    bias_ref[...] = jnp.where(dist >= 0, -slope2 * dist.astype(F32), MASK_VALUE)
    lam = (jnp.exp(jnp.sum(lq1_ref[...] * lk1_ref[...], axis=-1, keepdims=True))
           - jnp.exp(jnp.sum(lq2_ref[...] * lk2_ref[...], axis=-1, keepdims=True))
           + lambda_init)
    head_gain = hg_ref[...]

    def scores(qi):
        q0, n_keys = qi * tq, (qi + 1) * tq
        return [lax.dot_general(q_ref[q0:q0 + tq, comp * dh:(comp + 1) * dh],
                                k_ref[0:n_keys, comp * dh:(comp + 1) * dh],
                                nt_dims, preferred_element_type=F32) for comp in range(2)]

    def attend(qi, qk):
        q0, n_keys = qi * tq, (qi + 1) * tq
        v_blk = v_ref[0:n_keys, :]
        outs = []
        for comp in range(2):
            tiles = _lane_tiles(qk[comp] + bias_ref[:, SEQ - n_keys:])
            m = jnp.max(functools.reduce(jnp.maximum, tiles), axis=-1, keepdims=True)
            p_tiles = [jnp.exp2(ti - m) for ti in tiles]
            l = jnp.sum(functools.reduce(jnp.add, p_tiles), axis=-1, keepdims=True)
            p = jnp.concatenate([pt.astype(BF16) for pt in p_tiles], axis=1)
            outs.append(jnp.dot(p, v_blk, preferred_element_type=F32) / l)
        o = outs[0] - lam * outs[1]
        ms = jnp.mean(o * o, axis=-1, keepdims=True)
        o = ((o * lax.rsqrt(ms + EPS)) * head_gain) * (1.0 - lambda_init)
        o_ref[q0:q0 + tq, :] = o.astype(o_ref.dtype)

    order = list(reversed(range(SEQ // tq)))
    qk_next = scores(order[0])
    for n, qi in enumerate(order):
        qk = qk_next
        if n + 1 < len(order):
            qk_next = scores(order[n + 1])
        attend(qi, qk)


def _diff_attn_core(q, k, v, slopes, lq1, lk1, lq2, lk2, head_gain, lambda_init):
    blk = pl.BlockSpec((SEQ, DIFF_VDIM), lambda b, h: (b, h))
    vec = pl.BlockSpec((1, DIFF_HEAD_DIM), lambda b, h: (0, 0))
    return pl.pallas_call(
        functools.partial(_attn_kernel, lambda_init=lambda_init),
        grid=(BATCH, DIFF_HEADS),
        in_specs=[pl.BlockSpec((1, 1, LANES), lambda b, h: (h, 0, 0)),
                  blk, blk, blk, vec, vec, vec, vec,
                  pl.BlockSpec((1, DIFF_VDIM), lambda b, h: (0, 0))],
        out_specs=blk,
        out_shape=jax.ShapeDtypeStruct((TOKENS, DIFF_V), BF16),
        scratch_shapes=[pltpu.VMEM((ATT_TQ, SEQ), F32)],
        compiler_params=_params(2),
        name="diff_attn",
    )(slopes, q, k, v,
      lq1.reshape(1, DIFF_HEAD_DIM), lk1.reshape(1, DIFF_HEAD_DIM),
      lq2.reshape(1, DIFF_HEAD_DIM), lk2.reshape(1, DIFF_HEAD_DIM),
      head_gain.reshape(1, DIFF_VDIM))


CAST_BLOCK_BYTES = 8 * 1024 * 1024


def _cast_kernel(x_ref, o_ref):
    o_ref[...] = x_ref[...].astype(o_ref.dtype)


def _to_bf16(w, layer):
    _, r, c = w.shape
    bc = next((t for t in (2048, 1024) if c % t == 0), c)
    br = max(t for t in range(16, r + 1, 16) if r % t == 0 and t * bc * 4 <= CAST_BLOCK_BYTES)
    return pl.pallas_call(
        _cast_kernel,
        grid=(r // br, c // bc),
        in_specs=[pl.BlockSpec((None, br, bc), lambda i, j: (layer, i, j))],
        out_specs=pl.BlockSpec((br, bc), lambda i, j: (i, j)),
        out_shape=jax.ShapeDtypeStruct((r, c), BF16),
        compiler_params=_params(2),
        name="cast_bf16",
    )(w)


def kernel(x, gla_attn_norm, gla_w_in, gla_w_alpha2, gla_b_alpha, gla_head_norm, gla_w_out, kv_norm, w_kv, k_norm, diff_attn_norm, diff_w_q, diff_q_norm, diff_lambda_q1, diff_lambda_k1, diff_lambda_q2, diff_lambda_k2, diff_head_norm, diff_w_out, ffn_norm, ffn_w_in, ffn_w_out):
    xt = x.reshape(TOKENS, D_MODEL)
    pad = LANES - GLA_GATE_RANK
    slopes = (2.0 ** (-8.0 * jnp.arange(1, DIFF_HEADS + 1, dtype=F32) / DIFF_HEADS))
    slopes = jnp.broadcast_to(slopes[:, None, None], (DIFF_HEADS, 1, LANES))

    gla_in_t = jnp.swapaxes(gla_w_in, 1, 2)
    w_kv3 = w_kv[None]

    gla_in_b = {0: _to_bf16(gla_in_t, 0)}
    gla_out_b = {0: _to_bf16(gla_w_out, 0)}
    side_jobs = {
        0: [("ffn_out", ffn_w_out, 0), ("gla_in", gla_in_t, 1), ("gla_out", gla_w_out, 1)],
        1: [("ffn_out", ffn_w_out, 1), ("kv", w_kv3, 0), ("q", diff_w_q, 0), ("o", diff_w_out, 0)],
        2: [("ffn_out", ffn_w_out, 2), ("q", diff_w_q, 1), ("o", diff_w_out, 1)],
        3: [("ffn_out", ffn_w_out, 3)],
    }
    ready = {}

    k_shared = v_shared = None
    for l in range(DEPTH):
        if l < N_A_LAYERS:
            i = l
            w_in_b = gla_in_b[0][None] if i == 0 else ready["gla_in", i]
            w_out_b = gla_out_b[0][None] if i == 0 else ready["gla_out", i]
            w2 = jnp.pad(gla_w_alpha2[i], ((0, pad), (0, 0))).astype(BF16)
            proj, a_low = _norm_proj(xt, gla_attn_norm[i], w_in_b, 0, n=GLA_MAIN, bm=1024, bn=1024,
                                     out_dtype=F32, gate_rank=GLA_GATE_RANK, w_transposed=True)
            o = _gla_core(proj, a_low, w2, gla_b_alpha[i], gla_head_norm[i])
            xt, xn = _res_mm_norm(o, w_out_b, 0, xt, ffn_norm[l], bm=512)
        else:
            if l == N_A_LAYERS:
                k_shared = _norm_proj(xt, kv_norm, ready["kv", 0], 0, n=DIFF_QK, bm=1024, bn=1024,
                                      out_dtype=BF16, gn_gain=k_norm)
                v_shared = _norm_proj(xt, kv_norm, ready["kv", 0], 0, n=DIFF_V, col0=DIFF_QK, bm=1024, bn=1024,
                                      out_dtype=BF16)
            j = l - N_A_LAYERS
            lambda_init = 0.8 - 0.6 * math.exp(-0.3 * l)
            q = _norm_proj(xt, diff_attn_norm[j], ready["q", j], 0, n=DIFF_QK, bm=1024, bn=1024,
                           out_dtype=BF16, gn_gain=diff_q_norm[j], scale=DIFF_HEAD_DIM ** -0.5 * LOG2E)
            o = _diff_attn_core(q, k_shared, v_shared, slopes, diff_lambda_q1[j], diff_lambda_k1[j],
                                diff_lambda_q2[j], diff_lambda_k2[j], diff_head_norm[j], lambda_init)
            xt, xn = _res_mm_norm(o, ready["o", j], 0, xt, ffn_norm[l], bm=512)
        jobs = side_jobs[l]
        act, casted = _ffn_up(xn, ffn_w_in, l, [(w, idx) for _, w, idx in jobs], bm=1024, bn=512)
        for (name, _, idx), w_b in zip(jobs, casted):
            ready[name, idx] = w_b[None]
        xt = _res_mm(act, ready["ffn_out", l], 0, xt, bm=1024, bn=512)
    return xt.reshape(BATCH, SEQ, D_MODEL)
```

```python
import functools
import math

import jax
import jax.numpy as jnp
from jax import lax
from jax.experimental import pallas as pl
from jax.experimental.pallas import tpu as pltpu

D_MODEL = 2048
BATCH = 4
SEQ = 2048
DEPTH = 4
TOKENS = BATCH * SEQ
N_A_LAYERS = DEPTH // 2
N_B_LAYERS = DEPTH - N_A_LAYERS
EPS = 1e-6

GLA_HEADS = 4
GLA_DK = D_MODEL // 2
GLA_DV = D_MODEL
GLA_HK = GLA_DK // GLA_HEADS
GLA_HV = GLA_DV // GLA_HEADS
GLA_GATE_RANK = 16
GLA_GATE_TAU = 16.0
GLA_CHUNK = 64
GLA_MAIN = 2 * GLA_DK + 2 * GLA_DV

DIFF_HEAD_DIM = 128
DIFF_HEADS = D_MODEL // (2 * DIFF_HEAD_DIM)
DIFF_VDIM = 2 * DIFF_HEAD_DIM
DIFF_QK = DIFF_HEADS * 2 * DIFF_HEAD_DIM
DIFF_V = DIFF_HEADS * DIFF_VDIM

D_FF = -(-8 * D_MODEL // (3 * 256)) * 256

LANES = 128
MASK_VALUE = -1e30

BF16 = jnp.bfloat16
F32 = jnp.float32

VMEM_LIMIT = 56 * 1024 * 1024


def _params(n_axes):
    return pltpu.CompilerParams(dimension_semantics=("arbitrary",) * n_axes,
                                vmem_limit_bytes=VMEM_LIMIT)


def _silu(x):
    return x / (1.0 + jnp.exp(-x))


NORM_ROWS = 128


def _fill_normed(x_ref, g_ref, xn_ref):
    gain = g_ref[...]

    def body(i, carry):
        rows = pl.ds(pl.multiple_of(i * NORM_ROWS, NORM_ROWS), NORM_ROWS)
        xv = x_ref[rows, :]
        ms = jnp.mean(xv * xv, axis=-1, keepdims=True)
        xn_ref[rows, :] = ((xv * lax.rsqrt(ms + EPS)) * gain).astype(BF16)
        return carry

    lax.fori_loop(0, x_ref.shape[0] // NORM_ROWS, body, 0)


def _group_norm_store(y, gn_ref, o_ref, scale):
    gain = gn_ref[...]
    for g in range(y.shape[1] // LANES):
        yg = y[:, g * LANES:(g + 1) * LANES]
        ms = jnp.mean(yg * yg, axis=-1, keepdims=True)
        val = (yg * lax.rsqrt(ms + EPS)) * gain
        if scale != 1.0:
            val = val * scale
        o_ref[:, g * LANES:(g + 1) * LANES] = val.astype(o_ref.dtype)


NT_DIMS = (((1,), (1,)), ((), ()))


def _norm_proj_kernel(x_ref, g_ref, w_ref, *rest, group_norm, scale, gate_rank, w_transposed):
    rest = list(rest)
    w1_ref = rest.pop(0) if gate_rank else None
    gn_ref = rest.pop(0) if group_norm else None
    o_ref = rest.pop(0)
    a_ref = rest.pop(0) if gate_rank else None
    xn_ref = rest.pop(0)

    def project(w_blk):
        if w_transposed:
            return lax.dot_general(xn_ref[...], w_blk, NT_DIMS, preferred_element_type=F32)
        return jnp.dot(xn_ref[...], w_blk, preferred_element_type=F32)

    @pl.when(pl.program_id(1) == 0)
    def _():
        _fill_normed(x_ref, g_ref, xn_ref)
        if gate_rank:
            a_ref[:, :gate_rank] = project(w1_ref[...])
            a_ref[:, gate_rank:] = jnp.zeros((a_ref.shape[0], a_ref.shape[1] - gate_rank), F32)

    y = project(w_ref[...])
    if group_norm:
        _group_norm_store(y, gn_ref, o_ref, scale)
    else:
        o_ref[...] = y.astype(o_ref.dtype)


def _w_spec(k, bn, layer, col_block0):
    return pl.BlockSpec((None, k, bn), lambda i, j: (layer, 0, j + col_block0))


def _norm_proj(x, gain, w, layer, *, n, bm, bn, out_dtype, col0=0, gate_rank=0, gn_gain=None, scale=1.0,
               w_transposed=False):
    m, k = x.shape
    group_norm = gn_gain is not None
    assert not gate_rank or w_transposed
    if w_transposed:
        w_spec = pl.BlockSpec((None, bn, k), lambda i, j: (layer, j + col0 // bn, 0))
    else:
        w_spec = _w_spec(k, bn, layer, col0 // bn)
    in_specs = [pl.BlockSpec((bm, k), lambda i, j: (i, 0)),
                pl.BlockSpec((1, k), lambda i, j: (0, 0)),
                w_spec]
    args = [x, gain.reshape(1, k), w]
    if gate_rank:
        in_specs.append(pl.BlockSpec((None, gate_rank, k), lambda i, j: (layer, (col0 + n) // gate_rank, 0)))
        args.append(w)
    if group_norm:
        in_specs.append(pl.BlockSpec((1, LANES), lambda i, j: (0, 0)))
        args.append(gn_gain.reshape(1, LANES))
    out_shape = [jax.ShapeDtypeStruct((m, n), out_dtype)]
    out_specs = [pl.BlockSpec((bm, bn), lambda i, j: (i, j))]
    if gate_rank:
        out_shape.append(jax.ShapeDtypeStruct((m, LANES), F32))
        out_specs.append(pl.BlockSpec((bm, LANES), lambda i, j: (i, 0)))
    res = pl.pallas_call(
        functools.partial(_norm_proj_kernel, group_norm=group_norm, scale=scale, gate_rank=gate_rank,
                          w_transposed=w_transposed),
        grid=(m // bm, n // bn),
        in_specs=in_specs,
        out_specs=out_specs,
        out_shape=out_shape,
        scratch_shapes=[pltpu.VMEM((bm, k), BF16)],
        compiler_params=_params(2),
        name="norm_proj",
    )(*args)
    return res if gate_rank else res[0]


CAST_ROWS = 256


def _cast_tile(src_ref, dst_ref):
    def body(i, carry):
        rows = pl.ds(pl.multiple_of(i * CAST_ROWS, CAST_ROWS), CAST_ROWS)
        dst_ref[rows, :] = src_ref[rows, :].astype(dst_ref.dtype)
        return carry

    lax.fori_loop(0, src_ref.shape[0] // CAST_ROWS, body, 0)


def _ffn_up_kernel(xn_ref, wg_ref, wu_ref, *rest, n_side):
    side_src = rest[:n_side]
    o_ref = rest[n_side]
    side_dst = rest[n_side + 1:2 * n_side + 1]
    wgb_ref, wub_ref = rest[2 * n_side + 1:]

    @pl.when(pl.program_id(1) == 0)
    def _():
        _cast_tile(wg_ref, wgb_ref)
        _cast_tile(wu_ref, wub_ref)

    xn = xn_ref[...]
    half = o_ref.shape[1] // 2
    for cols in (slice(0, half), slice(half, 2 * half)):
        g = jnp.dot(xn, wgb_ref[:, cols], preferred_element_type=F32)
        u = jnp.dot(xn, wub_ref[:, cols], preferred_element_type=F32)
        o_ref[:, cols] = (_silu(g) * u).astype(o_ref.dtype)

    for src, dst in zip(side_src, side_dst):
        dst[...] = src[...].astype(dst.dtype)


def _ffn_up(xn, w_in, layer, side, *, bm, bn):
    m, k = xn.shape
    nblk, mblk = D_FF // bn, m // bm
    n_steps = nblk * mblk
    side_in, side_out, side_shapes = [], [], []
    for w, w_layer in side:
        _, r, c = w.shape
        rows = min(t for t in range(16, r + 1, 16) if r % t == 0 and r // t <= n_steps)
        last = r // rows - 1
        side_in.append(pl.BlockSpec(
            (None, rows, c), lambda j, i, w_layer=w_layer, last=last: (w_layer, jnp.minimum(j * mblk + i, last), 0)))
        side_out.append(pl.BlockSpec((rows, c), lambda j, i, last=last: (jnp.minimum(j * mblk + i, last), 0)))
        side_shapes.append(jax.ShapeDtypeStruct((r, c), BF16))
    res = pl.pallas_call(
        functools.partial(_ffn_up_kernel, n_side=len(side)),
        grid=(nblk, mblk),
        in_specs=[pl.BlockSpec((bm, k), lambda j, i: (i, 0)),
                  pl.BlockSpec((None, k, bn), lambda j, i: (layer, 0, j)),
                  pl.BlockSpec((None, k, bn), lambda j, i: (layer, 0, j + nblk))] + side_in,
        out_specs=[pl.BlockSpec((bm, bn), lambda j, i: (i, j))] + side_out,
        out_shape=[jax.ShapeDtypeStruct((m, D_FF), BF16)] + side_shapes,
        scratch_shapes=[pltpu.VMEM((k, bn), BF16), pltpu.VMEM((k, bn), BF16)],
        compiler_params=_params(2),
        name="ffn_up",
    )(xn, w_in, w_in, *[w for w, _ in side])
    return res[0], res[1:]


def _res_mm_norm_kernel(a_ref, w_ref, r_ref, g_ref, o_ref, xn_ref):
    o_ref[...] = r_ref[...] + jnp.dot(a_ref[...], w_ref[...], preferred_element_type=F32)
    _fill_normed(o_ref, g_ref, xn_ref)


def _res_mm_norm(a, w, layer, res, gain, *, bm):
    m, k = a.shape
    n = w.shape[2]
    row_blk = lambda i: (i, 0)
    return pl.pallas_call(
        _res_mm_norm_kernel,
        grid=(m // bm,),
        in_specs=[pl.BlockSpec((bm, k), row_blk),
                  pl.BlockSpec((None, k, n), lambda i: (layer, 0, 0)),
                  pl.BlockSpec((bm, n), row_blk),
                  pl.BlockSpec((1, n), lambda i: (0, 0))],
        out_specs=[pl.BlockSpec((bm, n), row_blk), pl.BlockSpec((bm, n), row_blk)],
        out_shape=[jax.ShapeDtypeStruct((m, n), F32), jax.ShapeDtypeStruct((m, n), BF16)],
        compiler_params=_params(1),
        name="res_mm_norm",
    )(a, w, res, gain.reshape(1, n))


def _res_mm_kernel(a_ref, w_ref, r_ref, o_ref):
    o_ref[...] = r_ref[...] + jnp.dot(a_ref[...], w_ref[...], preferred_element_type=F32)


def _res_mm(a, w, layer, res, *, bm, bn):
    m, k = a.shape
    n = w.shape[2]
    return pl.pallas_call(
        _res_mm_kernel,
        grid=(m // bm, n // bn),
        in_specs=[pl.BlockSpec((bm, k), lambda i, j: (i, 0)),
                  _w_spec(k, bn, layer, 0),
                  pl.BlockSpec((bm, bn), lambda i, j: (i, j))],
        out_specs=pl.BlockSpec((bm, bn), lambda i, j: (i, j)),
        out_shape=jax.ShapeDtypeStruct((m, n), F32),
        compiler_params=_params(2),
        name="res_mm",
    )(a, w, res)


GLA_ROWS = 512


def _split2(x):
    hi = x.astype(BF16)
    lo = (x - hi.astype(F32)).astype(BF16)
    return hi, lo


def _gla_kernel(p_ref, a_ref, w2_ref, b_ref, hg_ref, o_ref, state_ref):
    c = GLA_CHUNK

    @pl.when(pl.program_id(1) == 0)
    def _():
        state_ref[...] = jnp.zeros_like(state_ref)

    row = lax.broadcasted_iota(jnp.int32, (c, c), 0)
    col = lax.broadcasted_iota(jnp.int32, (c, c), 1)
    causal = row >= col
    tril = causal.astype(BF16)
    head_gain = hg_ref[...]
    tn_dims = (((0,), (0,)), ((), ()))
    nt_dims = (((1,), (1,)), ((), ()))

    heads = range(GLA_HEADS)
    hk = [slice(h * GLA_HK, (h + 1) * GLA_HK) for h in heads]
    hv = [slice(h * GLA_HV, (h + 1) * GLA_HV) for h in heads]

    def rows_of(ci):
        return slice(ci * c, (ci + 1) * c)

    def gate_preact(ci):
        a_low = a_ref[rows_of(ci), :].astype(BF16)
        return jnp.dot(a_low, w2_ref[...], preferred_element_type=F32) + b_ref[...]

    def cum_log_decay(z):
        log_a = (jnp.minimum(z, 0.0) - jnp.log(1.0 + jnp.exp(-jnp.abs(z)))) / GLA_GATE_TAU
        return sum(jnp.dot(tril, part, preferred_element_type=F32) for part in _split2(log_a))

    def decayed_operands(ci, bcum):
        b_last = bcum[c - 1:c, :]
        q = p_ref[rows_of(ci), 0:GLA_DK]
        k = p_ref[rows_of(ci), GLA_DK:2 * GLA_DK]
        q_dec = ((q * (GLA_HK ** -0.5)) * jnp.exp(bcum)).astype(BF16)
        k_inv = (k * jnp.exp(-bcum)).astype(BF16)
        k_end = (k * jnp.exp(b_last - bcum)).astype(BF16)
        return q_dec, k_inv, k_end, jnp.exp(b_last)

    n_chunks = GLA_ROWS // c
    prepared = decayed_operands(0, cum_log_decay(gate_preact(0)))
    for ci in range(n_chunks):
        rows = rows_of(ci)
        q_dec, k_inv, k_end, chunk_decay = prepared
        has_next = ci + 1 < n_chunks
        if has_next:
            z_next = gate_preact(ci + 1)
        v_b = p_ref[rows, 2 * GLA_DK:2 * GLA_DK + GLA_DV].astype(BF16)
        states = [state_ref[h] for h in heads]
        o_inter = [jnp.dot(q_dec[:, hk[h]], states[h].astype(BF16), preferred_element_type=F32) for h in heads]
        kv = [lax.dot_general(k_end[:, hk[h]], v_b[:, hv[h]], tn_dims, preferred_element_type=F32)
              for h in heads]
        attn = [lax.dot_general(q_dec[:, hk[h]], k_inv[:, hk[h]], nt_dims, preferred_element_type=F32)
                for h in heads]
        if has_next:
            bcum_next = cum_log_decay(z_next)
        o_intra = [jnp.dot(jnp.where(causal, attn[h], 0.0).astype(BF16), v_b[:, hv[h]],
                           preferred_element_type=F32) for h in heads]
        if has_next:
            prepared = decayed_operands(ci + 1, bcum_next)
        for h in heads:
            decay = jnp.broadcast_to(chunk_decay[:, hk[h]], (LANES, GLA_HK)).T
            state_ref[h] = states[h] * jnp.concatenate([decay] * (GLA_HV // LANES), axis=1) + kv[h]
            o = o_intra[h] + o_inter[h]
            ms = jnp.mean(o * o, axis=-1, keepdims=True)
            o = (o * lax.rsqrt(ms + EPS)) * head_gain
            r = p_ref[rows, 2 * GLA_DK + GLA_DV + h * GLA_HV:2 * GLA_DK + GLA_DV + (h + 1) * GLA_HV]
            o_ref[rows, hv[h]] = (o * _silu(r)).astype(o_ref.dtype)


def _gla_core(proj, a_low, w2, b_alpha, head_gain):
    nblk = SEQ // GLA_ROWS
    return pl.pallas_call(
        _gla_kernel,
        grid=(BATCH, nblk),
        in_specs=[pl.BlockSpec((GLA_ROWS, GLA_MAIN), lambda b, n: (b * nblk + n, 0)),
                  pl.BlockSpec((GLA_ROWS, LANES), lambda b, n: (b * nblk + n, 0)),
                  pl.BlockSpec((LANES, GLA_DK), lambda b, n: (0, 0)),
                  pl.BlockSpec((1, GLA_DK), lambda b, n: (0, 0)),
                  pl.BlockSpec((1, GLA_HV), lambda b, n: (0, 0))],
        out_specs=pl.BlockSpec((GLA_ROWS, GLA_DV), lambda b, n: (b * nblk + n, 0)),
        out_shape=jax.ShapeDtypeStruct((TOKENS, GLA_DV), BF16),
        scratch_shapes=[pltpu.VMEM((GLA_HEADS, GLA_HK, GLA_HV), F32)],
        compiler_params=_params(2),
        name="gla_core",
    )(proj, a_low, w2, b_alpha.reshape(1, GLA_DK), head_gain.reshape(1, GLA_HV))


ATT_TQ = 256
LOG2E = math.log2(math.e)


def _lane_tiles(x):
    return [x[:, i * LANES:(i + 1) * LANES] for i in range(x.shape[1] // LANES)]


def _attn_kernel(slope_ref, q_ref, k_ref, v_ref, lq1_ref, lk1_ref, lq2_ref, lk2_ref, hg_ref, o_ref,
                 qx_ref, kx_ref, *, lambda_init):
    tq, dh = ATT_TQ, DIFF_HEAD_DIM
    nt_dims = (((1,), (1,)), ((), ()))
    lam = (jnp.exp(jnp.sum(lq1_ref[...] * lk1_ref[...], axis=-1, keepdims=True))
           - jnp.exp(jnp.sum(lq2_ref[...] * lk2_ref[...], axis=-1, keepdims=True))
           + lambda_init)
    head_gain = hg_ref[...]

    slope2 = slope_ref[0] * LOG2E
    s_hi = slope2.astype(BF16).astype(F32)
    s_mid = (slope2 - s_hi).astype(BF16).astype(F32)
    s_lo = (slope2 - s_hi - s_mid).astype(BF16).astype(F32)
    lane = lax.broadcasted_iota(jnp.int32, (1, LANES), 1)
    q_feat = jnp.where(lane < 2, s_hi, jnp.where(lane < 4, s_mid, jnp.where(lane < 6, s_lo, 0.0)))
    pos = lax.broadcasted_iota(jnp.int32, (SEQ, LANES), 0)
    lane_k = lax.broadcasted_iota(jnp.int32, (SEQ, LANES), 1)
    pos_lo = jnp.bitwise_and(pos, 255)
    pos_piece = jnp.where(jnp.bitwise_and(lane_k, 1) == 0, pos - pos_lo, pos_lo)
    k_feat = jnp.where(lane_k < 6, pos_piece, 0).astype(F32).astype(BF16)
    q_feat = jnp.broadcast_to(q_feat, (SEQ, LANES)).astype(BF16)
    for comp in range(2):
        qx_ref[comp, :, 0:dh] = q_ref[:, comp * dh:(comp + 1) * dh]
        qx_ref[comp, :, dh:2 * dh] = q_feat
        kx_ref[comp, :, 0:dh] = k_ref[:, comp * dh:(comp + 1) * dh]
        kx_ref[comp, :, dh:2 * dh] = k_feat

    row = lax.broadcasted_iota(jnp.int32, (tq, tq), 0)
    col = lax.broadcasted_iota(jnp.int32, (tq, tq), 1)
    causal_tiles = _lane_tiles(row >= col)

    def scores(qi):
        q0, n_keys = qi * tq, (qi + 1) * tq
        return [lax.dot_general(qx_ref[comp, q0:q0 + tq, :], kx_ref[comp, 0:n_keys, :],
                                nt_dims, preferred_element_type=F32) for comp in range(2)]

    def attend(qi, qk):
        q0, n_keys = qi * tq, (qi + 1) * tq
        v_blk = v_ref[0:n_keys, :]
        n_diag = len(causal_tiles)
        outs = []
        for comp in range(2):
            tiles = _lane_tiles(qk[comp])
            tiles = tiles[:-n_diag] + [jnp.where(keep, ti, MASK_VALUE)
                                       for keep, ti in zip(causal_tiles, tiles[-n_diag:])]
            m = jnp.max(functools.reduce(jnp.maximum, tiles), axis=-1, keepdims=True)
            p_tiles = [jnp.exp2(ti - m) for ti in tiles]
            l = jnp.sum(functools.reduce(jnp.add, p_tiles), axis=-1, keepdims=True)
            p = jnp.concatenate([pt.astype(BF16) for pt in p_tiles], axis=1)
            outs.append(jnp.dot(p, v_blk, preferred_element_type=F32) / l)
        o = outs[0] - lam * outs[1]
        ms = jnp.mean(o * o, axis=-1, keepdims=True)
        o = ((o * lax.rsqrt(ms + EPS)) * head_gain) * (1.0 - lambda_init)
        o_ref[q0:q0 + tq, :] = o.astype(o_ref.dtype)

    order = list(reversed(range(SEQ // tq)))
    qk_next = scores(order[0])
    for n, qi in enumerate(order):
        qk = qk_next
        if n + 1 < len(order):
            qk_next = scores(order[n + 1])
        attend(qi, qk)


def _diff_attn_core(q, k, v, slopes, lq1, lk1, lq2, lk2, head_gain, lambda_init):
    blk = pl.BlockSpec((SEQ, DIFF_VDIM), lambda b, h: (b, h))
    vec = pl.BlockSpec((1, DIFF_HEAD_DIM), lambda b, h: (0, 0))
    return pl.pallas_call(
        functools.partial(_attn_kernel, lambda_init=lambda_init),
        grid=(BATCH, DIFF_HEADS),
        in_specs=[pl.BlockSpec((1, 1, LANES), lambda b, h: (h, 0, 0)),
                  blk, blk, blk, vec, vec, vec, vec,
                  pl.BlockSpec((1, DIFF_VDIM), lambda b, h: (0, 0))],
        out_specs=blk,
        out_shape=jax.ShapeDtypeStruct((TOKENS, DIFF_V), BF16),
        scratch_shapes=[pltpu.VMEM((2, SEQ, 2 * DIFF_HEAD_DIM), BF16),
                        pltpu.VMEM((2, SEQ, 2 * DIFF_HEAD_DIM), BF16)],
        compiler_params=_params(2),
        name="diff_attn",
    )(slopes, q, k, v,
      lq1.reshape(1, DIFF_HEAD_DIM), lk1.reshape(1, DIFF_HEAD_DIM),
      lq2.reshape(1, DIFF_HEAD_DIM), lk2.reshape(1, DIFF_HEAD_DIM),
      head_gain.reshape(1, DIFF_VDIM))


CAST_BLOCK_BYTES = 8 * 1024 * 1024


def _cast_kernel(x_ref, o_ref):
    o_ref[...] = x_ref[...].astype(o_ref.dtype)


def _to_bf16(w, layer):
    _, r, c = w.shape
    bc = next((t for t in (2048, 1024) if c % t == 0), c)
    br = max(t for t in range(16, r + 1, 16) if r % t == 0 and t * bc * 4 <= CAST_BLOCK_BYTES)
    return pl.pallas_call(
        _cast_kernel,
        grid=(r // br, c // bc),
        in_specs=[pl.BlockSpec((None, br, bc), lambda i, j: (layer, i, j))],
        out_specs=pl.BlockSpec((br, bc), lambda i, j: (i, j)),
        out_shape=jax.ShapeDtypeStruct((r, c), BF16),
        compiler_params=_params(2),
        name="cast_bf16",
    )(w)


def kernel(x, gla_attn_norm, gla_w_in, gla_w_alpha2, gla_b_alpha, gla_head_norm, gla_w_out, kv_norm, w_kv, k_norm, diff_attn_norm, diff_w_q, diff_q_norm, diff_lambda_q1, diff_lambda_k1, diff_lambda_q2, diff_lambda_k2, diff_head_norm, diff_w_out, ffn_norm, ffn_w_in, ffn_w_out):
    xt = x.reshape(TOKENS, D_MODEL)
    pad = LANES - GLA_GATE_RANK
    slopes = (2.0 ** (-8.0 * jnp.arange(1, DIFF_HEADS + 1, dtype=F32) / DIFF_HEADS))
    slopes = jnp.broadcast_to(slopes[:, None, None], (DIFF_HEADS, 1, LANES))

    gla_in_t = jnp.swapaxes(gla_w_in, 1, 2)
    w_kv3 = w_kv[None]

    gla_in_b = {0: _to_bf16(gla_in_t, 0)}
    gla_out_b = {0: _to_bf16(gla_w_out, 0)}
    side_jobs = {
        0: [("ffn_out", ffn_w_out, 0), ("gla_in", gla_in_t, 1), ("gla_out", gla_w_out, 1)],
        1: [("ffn_out", ffn_w_out, 1), ("kv", w_kv3, 0), ("q", diff_w_q, 0), ("o", diff_w_out, 0)],
        2: [("ffn_out", ffn_w_out, 2), ("q", diff_w_q, 1), ("o", diff_w_out, 1)],
        3: [("ffn_out", ffn_w_out, 3)],
    }
    ready = {}

    k_shared = v_shared = None
    for l in range(DEPTH):
        if l < N_A_LAYERS:
            i = l
            w_in_b = gla_in_b[0][None] if i == 0 else ready["gla_in", i]
            w_out_b = gla_out_b[0][None] if i == 0 else ready["gla_out", i]
            w2 = jnp.pad(gla_w_alpha2[i], ((0, pad), (0, 0))).astype(BF16)
            proj, a_low = _norm_proj(xt, gla_attn_norm[i], w_in_b, 0, n=GLA_MAIN, bm=1024, bn=1024,
                                     out_dtype=F32, gate_rank=GLA_GATE_RANK, w_transposed=True)
            o = _gla_core(proj, a_low, w2, gla_b_alpha[i], gla_head_norm[i])
            xt, xn = _res_mm_norm(o, w_out_b, 0, xt, ffn_norm[l], bm=512)
        else:
            if l == N_A_LAYERS:
                k_shared = _norm_proj(xt, kv_norm, ready["kv", 0], 0, n=DIFF_QK, bm=1024, bn=1024,
                                      out_dtype=BF16, gn_gain=k_norm)
                v_shared = _norm_proj(xt, kv_norm, ready["kv", 0], 0, n=DIFF_V, col0=DIFF_QK, bm=1024, bn=1024,
                                      out_dtype=BF16)
            j = l - N_A_LAYERS
            lambda_init = 0.8 - 0.6 * math.exp(-0.3 * l)
            q = _norm_proj(xt, diff_attn_norm[j], ready["q", j], 0, n=DIFF_QK, bm=1024, bn=1024,
                           out_dtype=BF16, gn_gain=diff_q_norm[j], scale=DIFF_HEAD_DIM ** -0.5 * LOG2E)
            o = _diff_attn_core(q, k_shared, v_shared, slopes, diff_lambda_q1[j], diff_lambda_k1[j],
                                diff_lambda_q2[j], diff_lambda_k2[j], diff_head_norm[j], lambda_init)
            xt, xn = _res_mm_norm(o, ready["o", j], 0, xt, ffn_norm[l], bm=512)
        jobs = side_jobs[l]
        act, casted = _ffn_up(xn, ffn_w_in, l, [(w, idx) for _, w, idx in jobs], bm=1024, bn=512)
        for (name, _, idx), w_b in zip(jobs, casted):
            ready[name, idx] = w_b[None]
        xt = _res_mm(act, ready["ffn_out", l], 0, xt, bm=1024, bn=512)
    return xt.reshape(BATCH, SEQ, D_MODEL)
```

```python
import functools
import math

import jax
import jax.numpy as jnp
from jax import lax
from jax.experimental import pallas as pl
from jax.experimental.pallas import tpu as pltpu

D_MODEL = 2048
BATCH = 4
SEQ = 2048
DEPTH = 4
TOKENS = BATCH * SEQ
N_A_LAYERS = DEPTH // 2
N_B_LAYERS = DEPTH - N_A_LAYERS
EPS = 1e-6

GLA_HEADS = 4
GLA_DK = D_MODEL // 2
GLA_DV = D_MODEL
GLA_HK = GLA_DK // GLA_HEADS
GLA_HV = GLA_DV // GLA_HEADS
GLA_GATE_RANK = 16
GLA_GATE_TAU = 16.0
GLA_CHUNK = 64
GLA_MAIN = 2 * GLA_DK + 2 * GLA_DV

DIFF_HEAD_DIM = 128
DIFF_HEADS = D_MODEL // (2 * DIFF_HEAD_DIM)
DIFF_VDIM = 2 * DIFF_HEAD_DIM
DIFF_QK = DIFF_HEADS * 2 * DIFF_HEAD_DIM
DIFF_V = DIFF_HEADS * DIFF_VDIM

D_FF = -(-8 * D_MODEL // (3 * 256)) * 256

LANES = 128
MASK_VALUE = -1e30

BF16 = jnp.bfloat16
F32 = jnp.float32

VMEM_LIMIT = 56 * 1024 * 1024


def _params(n_axes):
    return pltpu.CompilerParams(dimension_semantics=("arbitrary",) * n_axes,
                                vmem_limit_bytes=VMEM_LIMIT)


def _silu(x):
    return x / (1.0 + jnp.exp(-x))


NORM_ROWS = 128


def _fill_normed(x_ref, g_ref, xn_ref):
    gain = g_ref[...]

    def body(i, carry):
        rows = pl.ds(pl.multiple_of(i * NORM_ROWS, NORM_ROWS), NORM_ROWS)
        xv = x_ref[rows, :]
        ms = jnp.mean(xv * xv, axis=-1, keepdims=True)
        xn_ref[rows, :] = ((xv * lax.rsqrt(ms + EPS)) * gain).astype(BF16)
        return carry

    lax.fori_loop(0, x_ref.shape[0] // NORM_ROWS, body, 0, unroll=2)


def _group_norm_store(y, gn_ref, o_ref, scale):
    gain = gn_ref[...]
    for g in range(y.shape[1] // LANES):
        yg = y[:, g * LANES:(g + 1) * LANES]
        ms = jnp.mean(yg * yg, axis=-1, keepdims=True)
        val = (yg * lax.rsqrt(ms + EPS)) * gain
        if scale != 1.0:
            val = val * scale
        o_ref[:, g * LANES:(g + 1) * LANES] = val.astype(o_ref.dtype)


NT_DIMS = (((1,), (1,)), ((), ()))


def _norm_proj_kernel(x_ref, g_ref, w_ref, *rest, group_norm, scale, gate_rank, w_transposed):
    rest = list(rest)
    w1_ref = rest.pop(0) if gate_rank else None
    gn_ref = rest.pop(0) if group_norm else None
    o_ref = rest.pop(0)
    a_ref = rest.pop(0) if gate_rank else None
    xn_ref = rest.pop(0)

    def project(w_blk):
        if w_transposed:
            return lax.dot_general(xn_ref[...], w_blk, NT_DIMS, preferred_element_type=F32)
        return jnp.dot(xn_ref[...], w_blk, preferred_element_type=F32)

    @pl.when(pl.program_id(1) == 0)
    def _():
        _fill_normed(x_ref, g_ref, xn_ref)
        if gate_rank:
            a_ref[:, :gate_rank] = project(w1_ref[...])
            a_ref[:, gate_rank:] = jnp.zeros((a_ref.shape[0], a_ref.shape[1] - gate_rank), F32)

    y = project(w_ref[...])
    if group_norm:
        _group_norm_store(y, gn_ref, o_ref, scale)
    else:
        o_ref[...] = y.astype(o_ref.dtype)


def _w_spec(k, bn, layer, col_block0):
    return pl.BlockSpec((None, k, bn), lambda i, j: (layer, 0, j + col_block0))


def _norm_proj(x, gain, w, layer, *, n, bm, bn, out_dtype, col0=0, gate_rank=0, gn_gain=None, scale=1.0,
               w_transposed=False):
    m, k = x.shape
    group_norm = gn_gain is not None
    assert not gate_rank or w_transposed
    if w_transposed:
        w_spec = pl.BlockSpec((None, bn, k), lambda i, j: (layer, j + col0 // bn, 0))
    else:
        w_spec = _w_spec(k, bn, layer, col0 // bn)
    in_specs = [pl.BlockSpec((bm, k), lambda i, j: (i, 0)),
                pl.BlockSpec((1, k), lambda i, j: (0, 0)),
                w_spec]
    args = [x, gain.reshape(1, k), w]
    if gate_rank:
        in_specs.append(pl.BlockSpec((None, gate_rank, k), lambda i, j: (layer, (col0 + n) // gate_rank, 0)))
        args.append(w)
    if group_norm:
        in_specs.append(pl.BlockSpec((1, LANES), lambda i, j: (0, 0)))
        args.append(gn_gain.reshape(1, LANES))
    out_shape = [jax.ShapeDtypeStruct((m, n), out_dtype)]
    out_specs = [pl.BlockSpec((bm, bn), lambda i, j: (i, j))]
    if gate_rank:
        out_shape.append(jax.ShapeDtypeStruct((m, LANES), F32))
        out_specs.append(pl.BlockSpec((bm, LANES), lambda i, j: (i, 0)))
    res = pl.pallas_call(
        functools.partial(_norm_proj_kernel, group_norm=group_norm, scale=scale, gate_rank=gate_rank,
                          w_transposed=w_transposed),
        grid=(m // bm, n // bn),
        in_specs=in_specs,
        out_specs=out_specs,
        out_shape=out_shape,
        scratch_shapes=[pltpu.VMEM((bm, k), BF16)],
        compiler_params=_params(2),
        name="norm_proj",
    )(*args)
    return res if gate_rank else res[0]


def _kvq_kernel(x_ref, gkv_ref, gq_ref, wkv_ref, wq_ref, kn_ref, qn_ref, o_ref, xkv_ref, xq_ref, *,
                k_blocks, kv_blocks, q_scale):
    j = pl.program_id(1)

    @pl.when(j == 0)
    def _():
        _fill_normed(x_ref, gkv_ref, xkv_ref)
        _fill_normed(x_ref, gq_ref, xq_ref)

    @pl.when(j < k_blocks)
    def _():
        _group_norm_store(jnp.dot(xkv_ref[...], wkv_ref[...], preferred_element_type=F32), kn_ref, o_ref, 1.0)

    @pl.when(jnp.logical_and(j >= k_blocks, j < kv_blocks))
    def _():
        o_ref[...] = jnp.dot(xkv_ref[...], wkv_ref[...], preferred_element_type=F32).astype(o_ref.dtype)

    @pl.when(j >= kv_blocks)
    def _():
        _group_norm_store(jnp.dot(xq_ref[...], wq_ref[...], preferred_element_type=F32), qn_ref, o_ref, q_scale)


def _kvq_proj(x, kv_gain, q_gain, w_kv, w_q, k_norm, q_norm, q_scale, *, bm, bn):
    m, k = x.shape
    kv_blocks, q_blocks = w_kv.shape[2] // bn, w_q.shape[2] // bn
    vec = lambda width: pl.BlockSpec((1, width), lambda i, j: (0, 0))
    return pl.pallas_call(
        functools.partial(_kvq_kernel, k_blocks=DIFF_QK // bn, kv_blocks=kv_blocks, q_scale=q_scale),
        grid=(m // bm, kv_blocks + q_blocks),
        in_specs=[pl.BlockSpec((bm, k), lambda i, j: (i, 0)), vec(k), vec(k),
                  pl.BlockSpec((None, k, bn), lambda i, j: (0, 0, jnp.minimum(j, kv_blocks - 1))),
                  pl.BlockSpec((None, k, bn), lambda i, j: (0, 0, jnp.maximum(j - kv_blocks, 0))),
                  vec(LANES), vec(LANES)],
        out_specs=pl.BlockSpec((bm, bn), lambda i, j: (i, j)),
        out_shape=jax.ShapeDtypeStruct((m, (kv_blocks + q_blocks) * bn), BF16),
        scratch_shapes=[pltpu.VMEM((bm, k), BF16), pltpu.VMEM((bm, k), BF16)],
        compiler_params=_params(2),
        name="kvq_proj",
    )(x, kv_gain.reshape(1, k), q_gain.reshape(1, k), w_kv, w_q,
      k_norm.reshape(1, LANES), q_norm.reshape(1, LANES))


CAST_ROWS = 256


def _cast_tile(src_ref, dst_ref):
    def body(i, carry):
        rows = pl.ds(pl.multiple_of(i * CAST_ROWS, CAST_ROWS), CAST_ROWS)
        dst_ref[rows, :] = src_ref[rows, :].astype(dst_ref.dtype)
        return carry

    lax.fori_loop(0, src_ref.shape[0] // CAST_ROWS, body, 0)


def _ffn_up_kernel(xn_ref, wg_ref, wu_ref, *rest, n_side):
    side_src = rest[:n_side]
    o_ref = rest[n_side]
    side_dst = rest[n_side + 1:2 * n_side + 1]
    wgb_ref, wub_ref = rest[2 * n_side + 1:]

    @pl.when(pl.program_id(1) == 0)
    def _():
        _cast_tile(wg_ref, wgb_ref)
        _cast_tile(wu_ref, wub_ref)

    xn = xn_ref[...]
    half = o_ref.shape[1] // 2
    for cols in (slice(0, half), slice(half, 2 * half)):
        g = jnp.dot(xn, wgb_ref[:, cols], preferred_element_type=F32)
        u = jnp.dot(xn, wub_ref[:, cols], preferred_element_type=F32)
        o_ref[:, cols] = (_silu(g) * u).astype(o_ref.dtype)

    for src, dst in zip(side_src, side_dst):
        dst[...] = src[...].astype(dst.dtype)


def _ffn_up(xn, w_in, layer, side, *, bm, bn):
    m, k = xn.shape
    nblk, mblk = D_FF // bn, m // bm
    n_steps = nblk * mblk
    side_in, side_out, side_shapes = [], [], []
    for w, w_layer in side:
        _, r, c = w.shape
        rows = min(t for t in range(16, r + 1, 16) if r % t == 0 and r // t <= n_steps)
        last = r // rows - 1
        side_in.append(pl.BlockSpec(
            (None, rows, c), lambda j, i, w_layer=w_layer, last=last: (w_layer, jnp.minimum(j * mblk + i, last), 0)))
        side_out.append(pl.BlockSpec((rows, c), lambda j, i, last=last: (jnp.minimum(j * mblk + i, last), 0)))
        side_shapes.append(jax.ShapeDtypeStruct((r, c), BF16))
    res = pl.pallas_call(
        functools.partial(_ffn_up_kernel, n_side=len(side)),
        grid=(nblk, mblk),
        in_specs=[pl.BlockSpec((bm, k), lambda j, i: (i, 0)),
                  pl.BlockSpec((None, k, bn), lambda j, i: (layer, 0, j)),
                  pl.BlockSpec((None, k, bn), lambda j, i: (layer, 0, j + nblk))] + side_in,
        out_specs=[pl.BlockSpec((bm, bn), lambda j, i: (i, j))] + side_out,
        out_shape=[jax.ShapeDtypeStruct((m, D_FF), BF16)] + side_shapes,
        scratch_shapes=[pltpu.VMEM((k, bn), BF16), pltpu.VMEM((k, bn), BF16)],
        compiler_params=_params(2),
        name="ffn_up",
    )(xn, w_in, w_in, *[w for w, _ in side])
    return res[0], res[1:]


def _res_mm_norm_kernel(a_ref, w_ref, r_ref, g_ref, o_ref, xn_ref):
    o_ref[...] = r_ref[...] + jnp.dot(a_ref[...], w_ref[...], preferred_element_type=F32)
    _fill_normed(o_ref, g_ref, xn_ref)


def _res_mm_norm(a, w, layer, res, gain, *, bm):
    m, k = a.shape
    n = w.shape[2]
    row_blk = lambda i: (i, 0)
    return pl.pallas_call(
        _res_mm_norm_kernel,
        grid=(m // bm,),
        in_specs=[pl.BlockSpec((bm, k), row_blk),
                  pl.BlockSpec((None, k, n), lambda i: (layer, 0, 0)),
                  pl.BlockSpec((bm, n), row_blk),
                  pl.BlockSpec((1, n), lambda i: (0, 0))],
        out_specs=[pl.BlockSpec((bm, n), row_blk), pl.BlockSpec((bm, n), row_blk)],
        out_shape=[jax.ShapeDtypeStruct((m, n), F32), jax.ShapeDtypeStruct((m, n), BF16)],
        compiler_params=_params(1),
        name="res_mm_norm",
    )(a, w, res, gain.reshape(1, n))


def _res_mm_kernel(a_ref, w_ref, r_ref, o_ref):
    o_ref[...] = r_ref[...] + jnp.dot(a_ref[...], w_ref[...], preferred_element_type=F32)


def _res_mm(a, w, layer, res, *, bm, bn):
    m, k = a.shape
    n = w.shape[2]
    return pl.pallas_call(
        _res_mm_kernel,
        grid=(m // bm, n // bn),
        in_specs=[pl.BlockSpec((bm, k), lambda i, j: (i, 0)),
                  _w_spec(k, bn, layer, 0),
                  pl.BlockSpec((bm, bn), lambda i, j: (i, j))],
        out_specs=pl.BlockSpec((bm, bn), lambda i, j: (i, j)),
        out_shape=jax.ShapeDtypeStruct((m, n), F32),
        compiler_params=_params(2),
        name="res_mm",
    )(a, w, res)


GLA_ROWS = 512


def _split2(x):
    hi = x.astype(BF16)
    lo = (x - hi.astype(F32)).astype(BF16)
    return hi, lo


def _gla_kernel(p_ref, a_ref, w2_ref, b_ref, hg_ref, o_ref, state_ref):
    c = GLA_CHUNK

    @pl.when(pl.program_id(1) == 0)
    def _():
        state_ref[...] = jnp.zeros_like(state_ref)

    row = lax.broadcasted_iota(jnp.int32, (c, c), 0)
    col = lax.broadcasted_iota(jnp.int32, (c, c), 1)
    causal = row >= col
    tril = causal.astype(BF16)
    head_gain = hg_ref[...]
    tn_dims = (((0,), (0,)), ((), ()))
    nt_dims = (((1,), (1,)), ((), ()))

    heads = range(GLA_HEADS)
    hk = [slice(h * GLA_HK, (h + 1) * GLA_HK) for h in heads]
    hv = [slice(h * GLA_HV, (h + 1) * GLA_HV) for h in heads]

    def rows_of(ci):
        return slice(ci * c, (ci + 1) * c)

    def gate_preact(ci):
        a_low = a_ref[rows_of(ci), :].astype(BF16)
        return jnp.dot(a_low, w2_ref[...], preferred_element_type=F32) + b_ref[...]

    def cum_log_decay(z):
        log_a = (jnp.minimum(z, 0.0) - jnp.log(1.0 + jnp.exp(-jnp.abs(z)))) / GLA_GATE_TAU
        return sum(jnp.dot(tril, part, preferred_element_type=F32) for part in _split2(log_a))

    def decayed_operands(ci, bcum):
        b_last = bcum[c - 1:c, :]
        q = p_ref[rows_of(ci), 0:GLA_DK]
        k = p_ref[rows_of(ci), GLA_DK:2 * GLA_DK]
        q_dec = ((q * (GLA_HK ** -0.5)) * jnp.exp(bcum)).astype(BF16)
        k_inv = (k * jnp.exp(-bcum)).astype(BF16)
        k_end = (k * jnp.exp(b_last - bcum)).astype(BF16)
        return q_dec, k_inv, k_end, jnp.exp(b_last)

    n_chunks = GLA_ROWS // c
    prepared = decayed_operands(0, cum_log_decay(gate_preact(0)))
    for ci in range(n_chunks):
        rows = rows_of(ci)
        q_dec, k_inv, k_end, chunk_decay = prepared
        has_next = ci + 1 < n_chunks
        if has_next:
            z_next = gate_preact(ci + 1)
        v_b = p_ref[rows, 2 * GLA_DK:2 * GLA_DK + GLA_DV].astype(BF16)
        states = [state_ref[h] for h in heads]
        o_inter = [jnp.dot(q_dec[:, hk[h]], states[h].astype(BF16), preferred_element_type=F32) for h in heads]
        kv = [lax.dot_general(k_end[:, hk[h]], v_b[:, hv[h]], tn_dims, preferred_element_type=F32)
              for h in heads]
        attn = [lax.dot_general(q_dec[:, hk[h]], k_inv[:, hk[h]], nt_dims, preferred_element_type=F32)
                for h in heads]
        if has_next:
            bcum_next = cum_log_decay(z_next)
        o_intra = [jnp.dot(jnp.where(causal, attn[h], 0.0).astype(BF16), v_b[:, hv[h]],
                           preferred_element_type=F32) for h in heads]
        if has_next:
            prepared = decayed_operands(ci + 1, bcum_next)
        for h in heads:
            decay = jnp.broadcast_to(chunk_decay[:, hk[h]], (LANES, GLA_HK)).T
            state_ref[h] = states[h] * jnp.concatenate([decay] * (GLA_HV // LANES), axis=1) + kv[h]
            o = o_intra[h] + o_inter[h]
            ms = jnp.mean(o * o, axis=-1, keepdims=True)
            o = (o * lax.rsqrt(ms + EPS)) * head_gain
            r = p_ref[rows, 2 * GLA_DK + GLA_DV + h * GLA_HV:2 * GLA_DK + GLA_DV + (h + 1) * GLA_HV]
            o_ref[rows, hv[h]] = (o * _silu(r)).astype(o_ref.dtype)


def _gla_core(proj, a_low, w2, b_alpha, head_gain):
    nblk = SEQ // GLA_ROWS
    return pl.pallas_call(
        _gla_kernel,
        grid=(BATCH, nblk),
        in_specs=[pl.BlockSpec((GLA_ROWS, GLA_MAIN), lambda b, n: (b * nblk + n, 0)),
                  pl.BlockSpec((GLA_ROWS, LANES), lambda b, n: (b * nblk + n, 0)),
                  pl.BlockSpec((LANES, GLA_DK), lambda b, n: (0, 0)),
                  pl.BlockSpec((1, GLA_DK), lambda b, n: (0, 0)),
                  pl.BlockSpec((1, GLA_HV), lambda b, n: (0, 0))],
        out_specs=pl.BlockSpec((GLA_ROWS, GLA_DV), lambda b, n: (b * nblk + n, 0)),
        out_shape=jax.ShapeDtypeStruct((TOKENS, GLA_DV), BF16),
        scratch_shapes=[pltpu.VMEM((GLA_HEADS, GLA_HK, GLA_HV), F32)],
        compiler_params=_params(2),
        name="gla_core",
    )(proj, a_low, w2, b_alpha.reshape(1, GLA_DK), head_gain.reshape(1, GLA_HV))


ATT_TQ = 256
LOG2E = math.log2(math.e)


def _lane_tiles(x):
    return [x[:, i * LANES:(i + 1) * LANES] for i in range(x.shape[1] // LANES)]


def _attn_kernel(slope_ref, q_ref, k_ref, v_ref, lq1_ref, lk1_ref, lq2_ref, lk2_ref, hg_ref, o_ref,
                 qx_ref, kx_ref, *, lambda_init):
    tq, dh = ATT_TQ, DIFF_HEAD_DIM
    nt_dims = (((1,), (1,)), ((), ()))
    lam = (jnp.exp(jnp.sum(lq1_ref[...] * lk1_ref[...], axis=-1, keepdims=True))
           - jnp.exp(jnp.sum(lq2_ref[...] * lk2_ref[...], axis=-1, keepdims=True))
           + lambda_init)
    head_gain = hg_ref[...]

    slope2 = slope_ref[0] * LOG2E
    s_hi = slope2.astype(BF16).astype(F32)
    s_mid = (slope2 - s_hi).astype(BF16).astype(F32)
    s_lo = (slope2 - s_hi - s_mid).astype(BF16).astype(F32)
    lane = lax.broadcasted_iota(jnp.int32, (1, LANES), 1)
    q_feat = jnp.where(lane < 2, s_hi, jnp.where(lane < 4, s_mid, jnp.where(lane < 6, s_lo, 0.0)))
    pos = lax.broadcasted_iota(jnp.int32, (SEQ, LANES), 0)
    lane_k = lax.broadcasted_iota(jnp.int32, (SEQ, LANES), 1)
    pos_lo = jnp.bitwise_and(pos, 255)
    pos_piece = jnp.where(jnp.bitwise_and(lane_k, 1) == 0, pos - pos_lo, pos_lo)
    k_feat = jnp.where(lane_k < 6, pos_piece, 0).astype(F32).astype(BF16)
    q_feat = jnp.broadcast_to(q_feat, (SEQ, LANES)).astype(BF16)
    for comp in range(2):
        qx_ref[comp, :, 0:dh] = q_ref[:, comp * dh:(comp + 1) * dh]
        qx_ref[comp, :, dh:2 * dh] = q_feat
        kx_ref[comp, :, 0:dh] = k_ref[:, comp * dh:(comp + 1) * dh]
        kx_ref[comp, :, dh:2 * dh] = k_feat

    row = lax.broadcasted_iota(jnp.int32, (tq, tq), 0)
    col = lax.broadcasted_iota(jnp.int32, (tq, tq), 1)
    causal_tiles = _lane_tiles(row >= col)

    def scores(qi):
        q0, n_keys = qi * tq, (qi + 1) * tq
        return [lax.dot_general(qx_ref[comp, q0:q0 + tq, :], kx_ref[comp, 0:n_keys, :],
                                nt_dims, preferred_element_type=F32) for comp in range(2)]

    def attend(qi, qk):
        q0, n_keys = qi * tq, (qi + 1) * tq
        v_blk = v_ref[0:n_keys, :]
        n_diag = len(causal_tiles)
        outs = []
        for comp in range(2):
            tiles = _lane_tiles(qk[comp])
            tiles = tiles[:-n_diag] + [jnp.where(keep, ti, MASK_VALUE)
                                       for keep, ti in zip(causal_tiles, tiles[-n_diag:])]
            m = jnp.max(functools.reduce(jnp.maximum, tiles), axis=-1, keepdims=True)
            p_tiles = [jnp.exp2(ti - m) for ti in tiles]
            l = jnp.sum(functools.reduce(jnp.add, p_tiles), axis=-1, keepdims=True)
            p = jnp.concatenate([pt.astype(BF16) for pt in p_tiles], axis=1)
            outs.append(jnp.dot(p, v_blk, preferred_element_type=F32) / l)
        o = outs[0] - lam * outs[1]
        ms = jnp.mean(o * o, axis=-1, keepdims=True)
        o = ((o * lax.rsqrt(ms + EPS)) * head_gain) * (1.0 - lambda_init)
        o_ref[q0:q0 + tq, :] = o.astype(o_ref.dtype)

    order = list(reversed(range(SEQ // tq)))
    qk_next = scores(order[0])
    for n, qi in enumerate(order):
        qk = qk_next
        if n + 1 < len(order):
            qk_next = scores(order[n + 1])
        attend(qi, qk)


def _diff_attn_core(q, k, v, slopes, lq1, lk1, lq2, lk2, head_gain, lambda_init):
    def head_blk(first):
        return pl.BlockSpec((SEQ, DIFF_VDIM), lambda b, h: (b, first + h))

    (q, q_first), (k, k_first), (v, v_first) = q, k, v
    blk = head_blk(0)
    vec = pl.BlockSpec((1, DIFF_HEAD_DIM), lambda b, h: (0, 0))
    return pl.pallas_call(
        functools.partial(_attn_kernel, lambda_init=lambda_init),
        grid=(BATCH, DIFF_HEADS),
        in_specs=[pl.BlockSpec((1, 1, LANES), lambda b, h: (h, 0, 0)),
                  head_blk(q_first), head_blk(k_first), head_blk(v_first), vec, vec, vec, vec,
                  pl.BlockSpec((1, DIFF_VDIM), lambda b, h: (0, 0))],
        out_specs=blk,
        out_shape=jax.ShapeDtypeStruct((TOKENS, DIFF_V), BF16),
        scratch_shapes=[pltpu.VMEM((2, SEQ, 2 * DIFF_HEAD_DIM), BF16),
                        pltpu.VMEM((2, SEQ, 2 * DIFF_HEAD_DIM), BF16)],
        compiler_params=_params(2),
        name="diff_attn",
    )(slopes, q, k, v,
      lq1.reshape(1, DIFF_HEAD_DIM), lk1.reshape(1, DIFF_HEAD_DIM),
      lq2.reshape(1, DIFF_HEAD_DIM), lk2.reshape(1, DIFF_HEAD_DIM),
      head_gain.reshape(1, DIFF_VDIM))


CAST_BLOCK_BYTES = 8 * 1024 * 1024


def _cast_kernel(x_ref, o_ref):
    o_ref[...] = x_ref[...].astype(o_ref.dtype)


def _to_bf16(w, layer):
    _, r, c = w.shape
    bc = next((t for t in (2048, 1024) if c % t == 0), c)
    br = max(t for t in range(16, r + 1, 16) if r % t == 0 and t * bc * 4 <= CAST_BLOCK_BYTES)
    return pl.pallas_call(
        _cast_kernel,
        grid=(r // br, c // bc),
        in_specs=[pl.BlockSpec((None, br, bc), lambda i, j: (layer, i, j))],
        out_specs=pl.BlockSpec((br, bc), lambda i, j: (i, j)),
        out_shape=jax.ShapeDtypeStruct((r, c), BF16),
        compiler_params=_params(2),
        name="cast_bf16",
    )(w)


def kernel(x, gla_attn_norm, gla_w_in, gla_w_alpha2, gla_b_alpha, gla_head_norm, gla_w_out, kv_norm, w_kv, k_norm, diff_attn_norm, diff_w_q, diff_q_norm, diff_lambda_q1, diff_lambda_k1, diff_lambda_q2, diff_lambda_k2, diff_head_norm, diff_w_out, ffn_norm, ffn_w_in, ffn_w_out):
    xt = x.reshape(TOKENS, D_MODEL)
    pad = LANES - GLA_GATE_RANK
    slopes = (2.0 ** (-8.0 * jnp.arange(1, DIFF_HEADS + 1, dtype=F32) / DIFF_HEADS))
    slopes = jnp.broadcast_to(slopes[:, None, None], (DIFF_HEADS, 1, LANES))

    gla_in_t = jnp.swapaxes(gla_w_in, 1, 2)
    w_kv3 = w_kv[None]

    gla_in_b = {0: _to_bf16(gla_in_t, 0)}
    gla_out_b = {0: _to_bf16(gla_w_out, 0)}
    side_jobs = {
        0: [("ffn_out", ffn_w_out, 0), ("gla_in", gla_in_t, 1), ("gla_out", gla_w_out, 1)],
        1: [("ffn_out", ffn_w_out, 1), ("kv", w_kv3, 0), ("q", diff_w_q, 0), ("o", diff_w_out, 0)],
        2: [("ffn_out", ffn_w_out, 2), ("q", diff_w_q, 1), ("o", diff_w_out, 1)],
        3: [("ffn_out", ffn_w_out, 3)],
    }
    ready = {}

    k_shared = v_shared = None
    for l in range(DEPTH):
        if l < N_A_LAYERS:
            i = l
            w_in_b = gla_in_b[0][None] if i == 0 else ready["gla_in", i]
            w_out_b = gla_out_b[0][None] if i == 0 else ready["gla_out", i]
            w2 = jnp.pad(gla_w_alpha2[i], ((0, pad), (0, 0))).astype(BF16)
            proj, a_low = _norm_proj(xt, gla_attn_norm[i], w_in_b, 0, n=GLA_MAIN, bm=1024, bn=1024,
                                     out_dtype=F32, gate_rank=GLA_GATE_RANK, w_transposed=True)
            o = _gla_core(proj, a_low, w2, gla_b_alpha[i], gla_head_norm[i])
            xt, xn = _res_mm_norm(o, w_out_b, 0, xt, ffn_norm[l], bm=512)
        else:
            j = l - N_A_LAYERS
            lambda_init = 0.8 - 0.6 * math.exp(-0.3 * l)
            q_scale = DIFF_HEAD_DIM ** -0.5 * LOG2E
            if l == N_A_LAYERS:
                kvq = _kvq_proj(xt, kv_norm, diff_attn_norm[j], ready["kv", 0], ready["q", j], k_norm,
                                diff_q_norm[j], q_scale, bm=1024, bn=1024)
                k_shared, v_shared = (kvq, 0), (kvq, DIFF_HEADS)
                q = (kvq, 2 * DIFF_HEADS)
            else:
                q = (_norm_proj(xt, diff_attn_norm[j], ready["q", j], 0, n=DIFF_QK, bm=1024, bn=1024,
                                out_dtype=BF16, gn_gain=diff_q_norm[j], scale=q_scale), 0)
            o = _diff_attn_core(q, k_shared, v_shared, slopes, diff_lambda_q1[j], diff_lambda_k1[j],
                                diff_lambda_q2[j], diff_lambda_k2[j], diff_head_norm[j], lambda_init)
            xt, xn = _res_mm_norm(o, ready["o", j], 0, xt, ffn_norm[l], bm=512)
        jobs = side_jobs[l]
        act, casted = _ffn_up(xn, ffn_w_in, l, [(w, idx) for _, w, idx in jobs], bm=1024, bn=512)
        for (name, _, idx), w_b in zip(jobs, casted):
            ready[name, idx] = w_b[None]
        xt = _res_mm(act, ready["ffn_out", l], 0, xt, bm=1024, bn=512)
    return xt.reshape(BATCH, SEQ, D_MODEL)
```

```python
import functools
import math

import jax
import jax.numpy as jnp
from jax import lax
from jax.experimental import pallas as pl
from jax.experimental.pallas import tpu as pltpu

D_MODEL = 2048
BATCH = 4
SEQ = 2048
DEPTH = 4
TOKENS = BATCH * SEQ
N_A_LAYERS = DEPTH // 2
N_B_LAYERS = DEPTH - N_A_LAYERS
EPS = 1e-6

GLA_HEADS = 4
GLA_DK = D_MODEL // 2
GLA_DV = D_MODEL
GLA_HK = GLA_DK // GLA_HEADS
GLA_HV = GLA_DV // GLA_HEADS
GLA_GATE_RANK = 16
GLA_GATE_TAU = 16.0
GLA_CHUNK = 64
GLA_MAIN = 2 * GLA_DK + 2 * GLA_DV

DIFF_HEAD_DIM = 128
DIFF_HEADS = D_MODEL // (2 * DIFF_HEAD_DIM)
DIFF_VDIM = 2 * DIFF_HEAD_DIM
DIFF_QK = DIFF_HEADS * 2 * DIFF_HEAD_DIM
DIFF_V = DIFF_HEADS * DIFF_VDIM

D_FF = -(-8 * D_MODEL // (3 * 256)) * 256

LANES = 128
MASK_VALUE = -1e30

BF16 = jnp.bfloat16
F32 = jnp.float32

VMEM_LIMIT = 56 * 1024 * 1024


def _params(n_axes):
    return pltpu.CompilerParams(dimension_semantics=("arbitrary",) * n_axes,
                                vmem_limit_bytes=VMEM_LIMIT)


def _silu(x):
    return x / (1.0 + jnp.exp(-x))


def _vec(param, idx=None):
    if idx is None:
        return param.reshape(1, 1, -1), 0
    return param.reshape(param.shape[0], 1, param.shape[1]), idx


def _vec_spec(vec):
    arr, idx = vec
    return pl.BlockSpec((None, 1, arr.shape[2]), lambda *_: (idx, 0, 0))


NORM_ROWS = 128


def _fill_normed(x_ref, g_ref, xn_ref):
    gain = g_ref[...]

    def body(i, carry):
        rows = pl.ds(pl.multiple_of(i * NORM_ROWS, NORM_ROWS), NORM_ROWS)
        xv = x_ref[rows, :]
        ms = jnp.mean(xv * xv, axis=-1, keepdims=True)
        xn_ref[rows, :] = ((xv * lax.rsqrt(ms + EPS)) * gain).astype(BF16)
        return carry

    lax.fori_loop(0, x_ref.shape[0] // NORM_ROWS, body, 0, unroll=2)


def _group_norm_store(y, gn_ref, o_ref, scale):
    gain = gn_ref[...]
    for g in range(y.shape[1] // LANES):
        yg = y[:, g * LANES:(g + 1) * LANES]
        ms = jnp.mean(yg * yg, axis=-1, keepdims=True)
        val = (yg * lax.rsqrt(ms + EPS)) * gain
        if scale != 1.0:
            val = val * scale
        o_ref[:, g * LANES:(g + 1) * LANES] = val.astype(o_ref.dtype)


NT_DIMS = (((1,), (1,)), ((), ()))


def _norm_proj_kernel(x_ref, g_ref, w_ref, *rest, group_norm, scale, gate_rank, w_transposed):
    rest = list(rest)
    w1_ref = rest.pop(0) if gate_rank else None
    gn_ref = rest.pop(0) if group_norm else None
    o_ref = rest.pop(0)
    a_ref = rest.pop(0) if gate_rank else None
    xn_ref = rest.pop(0)

    def project(w_blk):
        if w_transposed:
            return lax.dot_general(xn_ref[...], w_blk, NT_DIMS, preferred_element_type=F32)
        return jnp.dot(xn_ref[...], w_blk, preferred_element_type=F32)

    @pl.when(pl.program_id(1) == 0)
    def _():
        _fill_normed(x_ref, g_ref, xn_ref)
        if gate_rank:
            a_ref[:, :gate_rank] = project(w1_ref[...])
            a_ref[:, gate_rank:] = jnp.zeros((a_ref.shape[0], a_ref.shape[1] - gate_rank), F32)

    y = project(w_ref[...])
    if group_norm:
        _group_norm_store(y, gn_ref, o_ref, scale)
    else:
        o_ref[...] = y.astype(o_ref.dtype)


def _w_spec(k, bn, layer, col_block0):
    return pl.BlockSpec((None, k, bn), lambda i, j: (layer, 0, j + col_block0))


def _norm_proj(x, gain, w, layer, *, n, bm, bn, out_dtype, col0=0, gate_rank=0, gn_gain=None, scale=1.0,
               w_transposed=False):
    m, k = x.shape
    group_norm = gn_gain is not None
    assert not gate_rank or w_transposed
    if w_transposed:
        w_spec = pl.BlockSpec((None, bn, k), lambda i, j: (layer, j + col0 // bn, 0))
    else:
        w_spec = _w_spec(k, bn, layer, col0 // bn)
    in_specs = [pl.BlockSpec((bm, k), lambda i, j: (i, 0)), _vec_spec(gain), w_spec]
    args = [x, gain[0], w]
    if gate_rank:
        in_specs.append(pl.BlockSpec((None, gate_rank, k), lambda i, j: (layer, (col0 + n) // gate_rank, 0)))
        args.append(w)
    if group_norm:
        in_specs.append(_vec_spec(gn_gain))
        args.append(gn_gain[0])
    out_shape = [jax.ShapeDtypeStruct((m, n), out_dtype)]
    out_specs = [pl.BlockSpec((bm, bn), lambda i, j: (i, j))]
    if gate_rank:
        out_shape.append(jax.ShapeDtypeStruct((m, LANES), F32))
        out_specs.append(pl.BlockSpec((bm, LANES), lambda i, j: (i, 0)))
    res = pl.pallas_call(
        functools.partial(_norm_proj_kernel, group_norm=group_norm, scale=scale, gate_rank=gate_rank,
                          w_transposed=w_transposed),
        grid=(m // bm, n // bn),
        in_specs=in_specs,
        out_specs=out_specs,
        out_shape=out_shape,
        scratch_shapes=[pltpu.VMEM((bm, k), BF16)],
        compiler_params=_params(2),
        name="norm_proj",
    )(*args)
    return res if gate_rank else res[0]


def _kvq_kernel(x_ref, gkv_ref, gq_ref, wkv_ref, wq_ref, kn_ref, qn_ref, o_ref, xkv_ref, xq_ref, *,
                k_blocks, kv_blocks, q_scale):
    j = pl.program_id(1)

    @pl.when(j == 0)
    def _():
        _fill_normed(x_ref, gkv_ref, xkv_ref)
        _fill_normed(x_ref, gq_ref, xq_ref)

    @pl.when(j < k_blocks)
    def _():
        _group_norm_store(jnp.dot(xkv_ref[...], wkv_ref[...], preferred_element_type=F32), kn_ref, o_ref, 1.0)

    @pl.when(jnp.logical_and(j >= k_blocks, j < kv_blocks))
    def _():
        o_ref[...] = jnp.dot(xkv_ref[...], wkv_ref[...], preferred_element_type=F32).astype(o_ref.dtype)

    @pl.when(j >= kv_blocks)
    def _():
        _group_norm_store(jnp.dot(xq_ref[...], wq_ref[...], preferred_element_type=F32), qn_ref, o_ref, q_scale)


def _kvq_proj(x, kv_gain, q_gain, w_kv, w_q, k_norm, q_norm, q_scale, *, bm, bn):
    m, k = x.shape
    kv_blocks, q_blocks = w_kv.shape[2] // bn, w_q.shape[2] // bn
    return pl.pallas_call(
        functools.partial(_kvq_kernel, k_blocks=DIFF_QK // bn, kv_blocks=kv_blocks, q_scale=q_scale),
        grid=(m // bm, kv_blocks + q_blocks),
        in_specs=[pl.BlockSpec((bm, k), lambda i, j: (i, 0)), _vec_spec(kv_gain), _vec_spec(q_gain),
                  pl.BlockSpec((None, k, bn), lambda i, j: (0, 0, jnp.minimum(j, kv_blocks - 1))),
                  pl.BlockSpec((None, k, bn), lambda i, j: (0, 0, jnp.maximum(j - kv_blocks, 0))),
                  _vec_spec(k_norm), _vec_spec(q_norm)],
        out_specs=pl.BlockSpec((bm, bn), lambda i, j: (i, j)),
        out_shape=jax.ShapeDtypeStruct((m, (kv_blocks + q_blocks) * bn), BF16),
        scratch_shapes=[pltpu.VMEM((bm, k), BF16), pltpu.VMEM((bm, k), BF16)],
        compiler_params=_params(2),
        name="kvq_proj",
    )(x, kv_gain[0], q_gain[0], w_kv, w_q, k_norm[0], q_norm[0])


CAST_ROWS = 256


def _cast_tile(src_ref, dst_ref):
    def body(i, carry):
        rows = pl.ds(pl.multiple_of(i * CAST_ROWS, CAST_ROWS), CAST_ROWS)
        dst_ref[rows, :] = src_ref[rows, :].astype(dst_ref.dtype)
        return carry

    lax.fori_loop(0, src_ref.shape[0] // CAST_ROWS, body, 0)


def _ffn_up_kernel(xn_ref, wg_ref, wu_ref, *rest, n_side):
    side_src = rest[:n_side]
    o_ref = rest[n_side]
    side_dst = rest[n_side + 1:2 * n_side + 1]
    wgb_ref, wub_ref = rest[2 * n_side + 1:]

    @pl.when(pl.program_id(1) == 0)
    def _():
        _cast_tile(wg_ref, wgb_ref)
        _cast_tile(wu_ref, wub_ref)

    xn = xn_ref[...]
    half = o_ref.shape[1] // 2
    for cols in (slice(0, half), slice(half, 2 * half)):
        g = jnp.dot(xn, wgb_ref[:, cols], preferred_element_type=F32)
        u = jnp.dot(xn, wub_ref[:, cols], preferred_element_type=F32)
        o_ref[:, cols] = (_silu(g) * u).astype(o_ref.dtype)

    for src, dst in zip(side_src, side_dst):
        dst[...] = src[...].astype(dst.dtype)


def _ffn_up(xn, w_in, layer, side, *, bm, bn):
    m, k = xn.shape
    nblk, mblk = D_FF // bn, m // bm
    n_steps = nblk * mblk
    side_in, side_out, side_shapes = [], [], []
    for w, w_layer in side:
        _, r, c = w.shape
        rows = min(t for t in range(16, r + 1, 16) if r % t == 0 and r // t <= n_steps)
        last = r // rows - 1
        side_in.append(pl.BlockSpec(
            (None, rows, c), lambda j, i, w_layer=w_layer, last=last: (w_layer, jnp.minimum(j * mblk + i, last), 0)))
        side_out.append(pl.BlockSpec((rows, c), lambda j, i, last=last: (jnp.minimum(j * mblk + i, last), 0)))
        side_shapes.append(jax.ShapeDtypeStruct((r, c), BF16))
    res = pl.pallas_call(
        functools.partial(_ffn_up_kernel, n_side=len(side)),
        grid=(nblk, mblk),
        in_specs=[pl.BlockSpec((bm, k), lambda j, i: (i, 0)),
                  pl.BlockSpec((None, k, bn), lambda j, i: (layer, 0, j)),
                  pl.BlockSpec((None, k, bn), lambda j, i: (layer, 0, j + nblk))] + side_in,
        out_specs=[pl.BlockSpec((bm, bn), lambda j, i: (i, j))] + side_out,
        out_shape=[jax.ShapeDtypeStruct((m, D_FF), BF16)] + side_shapes,
        scratch_shapes=[pltpu.VMEM((k, bn), BF16), pltpu.VMEM((k, bn), BF16)],
        compiler_params=_params(2),
        name="ffn_up",
    )(xn, w_in, w_in, *[w for w, _ in side])
    return res[0], res[1:]


def _res_mm_norm_kernel(a_ref, w_ref, r_ref, g_ref, o_ref, xn_ref):
    o_ref[...] = r_ref[...] + jnp.dot(a_ref[...], w_ref[...], preferred_element_type=F32)
    _fill_normed(o_ref, g_ref, xn_ref)


def _res_mm_norm(a, w, layer, res, gain, *, bm):
    m, k = a.shape
    n = w.shape[2]
    row_blk = lambda i: (i, 0)
    return pl.pallas_call(
        _res_mm_norm_kernel,
        grid=(m // bm,),
        in_specs=[pl.BlockSpec((bm, k), row_blk),
                  pl.BlockSpec((None, k, n), lambda i: (layer, 0, 0)),
                  pl.BlockSpec((bm, n), row_blk),
                  _vec_spec(gain)],
        out_specs=[pl.BlockSpec((bm, n), row_blk), pl.BlockSpec((bm, n), row_blk)],
        out_shape=[jax.ShapeDtypeStruct((m, n), F32), jax.ShapeDtypeStruct((m, n), BF16)],
        compiler_params=_params(1),
        name="res_mm_norm",
    )(a, w, res, gain[0])


def _res_mm_kernel(a_ref, w_ref, r_ref, o_ref):
    o_ref[...] = r_ref[...] + jnp.dot(a_ref[...], w_ref[...], preferred_element_type=F32)


def _res_mm(a, w, layer, res, *, bm, bn):
    m, k = a.shape
    n = w.shape[2]
    return pl.pallas_call(
        _res_mm_kernel,
        grid=(m // bm, n // bn),
        in_specs=[pl.BlockSpec((bm, k), lambda i, j: (i, 0)),
                  _w_spec(k, bn, layer, 0),
                  pl.BlockSpec((bm, bn), lambda i, j: (i, j))],
        out_specs=pl.BlockSpec((bm, bn), lambda i, j: (i, j)),
        out_shape=jax.ShapeDtypeStruct((m, n), F32),
        compiler_params=_params(2),
        name="res_mm",
    )(a, w, res)


GLA_ROWS = 512


def _split2(x):
    hi = x.astype(BF16)
    lo = (x - hi.astype(F32)).astype(BF16)
    return hi, lo


def _gla_kernel(p_ref, a_ref, w2_ref, b_ref, hg_ref, o_ref, state_ref):
    c = GLA_CHUNK

    @pl.when(pl.program_id(1) == 0)
    def _():
        state_ref[...] = jnp.zeros_like(state_ref)

    row = lax.broadcasted_iota(jnp.int32, (c, c), 0)
    col = lax.broadcasted_iota(jnp.int32, (c, c), 1)
    causal = row >= col
    tril = causal.astype(BF16)
    head_gain = hg_ref[...]
    tn_dims = (((0,), (0,)), ((), ()))
    nt_dims = (((1,), (1,)), ((), ()))

    heads = range(GLA_HEADS)
    hk = [slice(h * GLA_HK, (h + 1) * GLA_HK) for h in heads]
    hv = [slice(h * GLA_HV, (h + 1) * GLA_HV) for h in heads]

    def rows_of(ci):
        return slice(ci * c, (ci + 1) * c)

    def gate_preact(ci):
        a_low = a_ref[rows_of(ci), :].astype(BF16)
        return jnp.dot(a_low, w2_ref[...], preferred_element_type=F32) + b_ref[...]

    def cum_log_decay(z):
        log_a = (jnp.minimum(z, 0.0) - jnp.log(1.0 + jnp.exp(-jnp.abs(z)))) / GLA_GATE_TAU
        return sum(jnp.dot(tril, part, preferred_element_type=F32) for part in _split2(log_a))

    def decayed_operands(ci, bcum):
        b_last = bcum[c - 1:c, :]
        q = p_ref[rows_of(ci), 0:GLA_DK]
        k = p_ref[rows_of(ci), GLA_DK:2 * GLA_DK]
        q_dec = ((q * (GLA_HK ** -0.5)) * jnp.exp(bcum)).astype(BF16)
        k_inv = (k * jnp.exp(-bcum)).astype(BF16)
        k_end = (k * jnp.exp(b_last - bcum)).astype(BF16)
        return q_dec, k_inv, k_end, jnp.exp(b_last)

    n_chunks = GLA_ROWS // c
    prepared = decayed_operands(0, cum_log_decay(gate_preact(0)))
    for ci in range(n_chunks):
        rows = rows_of(ci)
        q_dec, k_inv, k_end, chunk_decay = prepared
        has_next = ci + 1 < n_chunks
        if has_next:
            z_next = gate_preact(ci + 1)
        v_b = p_ref[rows, 2 * GLA_DK:2 * GLA_DK + GLA_DV].astype(BF16)
        states = [state_ref[h] for h in heads]
        o_inter = [jnp.dot(q_dec[:, hk[h]], states[h].astype(BF16), preferred_element_type=F32) for h in heads]
        kv = [lax.dot_general(k_end[:, hk[h]], v_b[:, hv[h]], tn_dims, preferred_element_type=F32)
              for h in heads]
        attn = [lax.dot_general(q_dec[:, hk[h]], k_inv[:, hk[h]], nt_dims, preferred_element_type=F32)
                for h in heads]
        if has_next:
            bcum_next = cum_log_decay(z_next)
        o_intra = [jnp.dot(jnp.where(causal, attn[h], 0.0).astype(BF16), v_b[:, hv[h]],
                           preferred_element_type=F32) for h in heads]
        if has_next:
            prepared = decayed_operands(ci + 1, bcum_next)
        for h in heads:
            decay = jnp.broadcast_to(chunk_decay[:, hk[h]], (LANES, GLA_HK)).T
            state_ref[h] = states[h] * jnp.concatenate([decay] * (GLA_HV // LANES), axis=1) + kv[h]
            o = o_intra[h] + o_inter[h]
            ms = jnp.mean(o * o, axis=-1, keepdims=True)
            o = (o * lax.rsqrt(ms + EPS)) * head_gain
            r = p_ref[rows, 2 * GLA_DK + GLA_DV + h * GLA_HV:2 * GLA_DK + GLA_DV + (h + 1) * GLA_HV]
            o_ref[rows, hv[h]] = (o * _silu(r)).astype(o_ref.dtype)


def _gla_core(proj, a_low, w2, b_alpha, head_gain):
    nblk = SEQ // GLA_ROWS
    return pl.pallas_call(
        _gla_kernel,
        grid=(BATCH, nblk),
        in_specs=[pl.BlockSpec((GLA_ROWS, GLA_MAIN), lambda b, n: (b * nblk + n, 0)),
                  pl.BlockSpec((GLA_ROWS, LANES), lambda b, n: (b * nblk + n, 0)),
                  pl.BlockSpec((LANES, GLA_DK), lambda b, n: (0, 0)),
                  _vec_spec(b_alpha), _vec_spec(head_gain)],
        out_specs=pl.BlockSpec((GLA_ROWS, GLA_DV), lambda b, n: (b * nblk + n, 0)),
        out_shape=jax.ShapeDtypeStruct((TOKENS, GLA_DV), BF16),
        scratch_shapes=[pltpu.VMEM((GLA_HEADS, GLA_HK, GLA_HV), F32)],
        compiler_params=_params(2),
        name="gla_core",
    )(proj, a_low, w2, b_alpha[0], head_gain[0])


ATT_TQ = 256
LOG2E = math.log2(math.e)


def _lane_tiles(x):
    return [x[:, i * LANES:(i + 1) * LANES] for i in range(x.shape[1] // LANES)]


def _attn_kernel(slope_ref, q_ref, k_ref, v_ref, lq1_ref, lk1_ref, lq2_ref, lk2_ref, hg_ref, o_ref,
                 qx_ref, kx_ref, *, lambda_init):
    tq, dh = ATT_TQ, DIFF_HEAD_DIM
    nt_dims = (((1,), (1,)), ((), ()))
    lam = (jnp.exp(jnp.sum(lq1_ref[...] * lk1_ref[...], axis=-1, keepdims=True))
           - jnp.exp(jnp.sum(lq2_ref[...] * lk2_ref[...], axis=-1, keepdims=True))
           + lambda_init)
    out_gain = hg_ref[...] * (1.0 - lambda_init)

    slope2 = slope_ref[0] * LOG2E
    s_hi = slope2.astype(BF16).astype(F32)
    s_mid = (slope2 - s_hi).astype(BF16).astype(F32)
    s_lo = (slope2 - s_hi - s_mid).astype(BF16).astype(F32)
    lane = lax.broadcasted_iota(jnp.int32, (1, LANES), 1)
    q_feat = jnp.where(lane < 2, s_hi, jnp.where(lane < 4, s_mid, jnp.where(lane < 6, s_lo, 0.0)))
    pos = lax.broadcasted_iota(jnp.int32, (SEQ, LANES), 0)
    lane_k = lax.broadcasted_iota(jnp.int32, (SEQ, LANES), 1)
    pos_lo = jnp.bitwise_and(pos, 255)
    pos_piece = jnp.where(jnp.bitwise_and(lane_k, 1) == 0, pos - pos_lo, pos_lo)
    k_feat = jnp.where(lane_k < 6, pos_piece, 0).astype(F32).astype(BF16)
    q_feat = jnp.broadcast_to(q_feat, (SEQ, LANES)).astype(BF16)
    for comp in range(2):
        qx_ref[comp, :, 0:dh] = q_ref[:, comp * dh:(comp + 1) * dh]
        qx_ref[comp, :, dh:2 * dh] = q_feat
        kx_ref[comp, :, 0:dh] = k_ref[:, comp * dh:(comp + 1) * dh]
        kx_ref[comp, :, dh:2 * dh] = k_feat

    row = lax.broadcasted_iota(jnp.int32, (tq, tq), 0)
    col = lax.broadcasted_iota(jnp.int32, (tq, tq), 1)
    causal_tiles = _lane_tiles(row >= col)

    def scores(qi):
        q0, n_keys = qi * tq, (qi + 1) * tq
        return [lax.dot_general(qx_ref[comp, q0:q0 + tq, :], kx_ref[comp, 0:n_keys, :],
                                nt_dims, preferred_element_type=F32) for comp in range(2)]

    def attend(qi, qk):
        q0, n_keys = qi * tq, (qi + 1) * tq
        v_blk = v_ref[0:n_keys, :]
        n_diag = len(causal_tiles)
        outs = []
        for comp in range(2):
            tiles = _lane_tiles(qk[comp])
            tiles = tiles[:-n_diag] + [jnp.where(keep, ti, MASK_VALUE)
                                       for keep, ti in zip(causal_tiles, tiles[-n_diag:])]
            m = jnp.max(functools.reduce(jnp.maximum, tiles), axis=-1, keepdims=True)
            p_tiles = [jnp.exp2(ti - m) for ti in tiles]
            l = jnp.sum(functools.reduce(jnp.add, p_tiles), axis=-1, keepdims=True)
            p = jnp.concatenate([pt.astype(BF16) for pt in p_tiles], axis=1)
            outs.append(jnp.dot(p, v_blk, preferred_element_type=F32) / l)
        o = outs[0] - lam * outs[1]
        ms = jnp.mean(o * o, axis=-1, keepdims=True)
        o = (o * lax.rsqrt(ms + EPS)) * out_gain
        o_ref[q0:q0 + tq, :] = o.astype(o_ref.dtype)

    order = list(reversed(range(SEQ // tq)))
    qk_next = scores(order[0])
    for n, qi in enumerate(order):
        qk = qk_next
        if n + 1 < len(order):
            qk_next = scores(order[n + 1])
        attend(qi, qk)


def _diff_attn_core(q, k, v, slopes, lq1, lk1, lq2, lk2, head_gain, lambda_init):
    def head_blk(first):
        return pl.BlockSpec((SEQ, DIFF_VDIM), lambda b, h: (b, first + h))

    (q, q_first), (k, k_first), (v, v_first) = q, k, v
    vecs = [lq1, lk1, lq2, lk2, head_gain]
    return pl.pallas_call(
        functools.partial(_attn_kernel, lambda_init=lambda_init),
        grid=(BATCH, DIFF_HEADS),
        in_specs=[pl.BlockSpec((1, 1, LANES), lambda b, h: (h, 0, 0)),
                  head_blk(q_first), head_blk(k_first), head_blk(v_first)] + [_vec_spec(p) for p in vecs],
        out_specs=head_blk(0),
        out_shape=jax.ShapeDtypeStruct((TOKENS, DIFF_V), BF16),
        scratch_shapes=[pltpu.VMEM((2, SEQ, 2 * DIFF_HEAD_DIM), BF16),
                        pltpu.VMEM((2, SEQ, 2 * DIFF_HEAD_DIM), BF16)],
        compiler_params=_params(2),
        name="diff_attn",
    )(slopes, q, k, v, *[p[0] for p in vecs])


CAST_BLOCK_BYTES = 8 * 1024 * 1024


def _cast_kernel(x_ref, o_ref):
    o_ref[...] = x_ref[...].astype(o_ref.dtype)


def _to_bf16(w, layer):
    _, r, c = w.shape
    bc = next((t for t in (2048, 1024) if c % t == 0), c)
    br = max(t for t in range(16, r + 1, 16) if r % t == 0 and t * bc * 4 <= CAST_BLOCK_BYTES)
    return pl.pallas_call(
        _cast_kernel,
        grid=(r // br, c // bc),
        in_specs=[pl.BlockSpec((None, br, bc), lambda i, j: (layer, i, j))],
        out_specs=pl.BlockSpec((br, bc), lambda i, j: (i, j)),
        out_shape=jax.ShapeDtypeStruct((r, c), BF16),
        compiler_params=_params(2),
        name="cast_bf16",
    )(w)


def kernel(x, gla_attn_norm, gla_w_in, gla_w_alpha2, gla_b_alpha, gla_head_norm, gla_w_out, kv_norm, w_kv, k_norm, diff_attn_norm, diff_w_q, diff_q_norm, diff_lambda_q1, diff_lambda_k1, diff_lambda_q2, diff_lambda_k2, diff_head_norm, diff_w_out, ffn_norm, ffn_w_in, ffn_w_out):
    xt = x.reshape(TOKENS, D_MODEL)
    pad = LANES - GLA_GATE_RANK
    slopes = (2.0 ** (-8.0 * jnp.arange(1, DIFF_HEADS + 1, dtype=F32) / DIFF_HEADS))
    slopes = jnp.broadcast_to(slopes[:, None, None], (DIFF_HEADS, 1, LANES))

    gla_in_t = jnp.swapaxes(gla_w_in, 1, 2)
    w_kv3 = w_kv[None]

    gla_in_b = {0: _to_bf16(gla_in_t, 0)}
    gla_out_b = {0: _to_bf16(gla_w_out, 0)}
    side_jobs = {
        0: [("ffn_out", ffn_w_out, 0), ("gla_in", gla_in_t, 1), ("gla_out", gla_w_out, 1)],
        1: [("ffn_out", ffn_w_out, 1), ("kv", w_kv3, 0), ("q", diff_w_q, 0), ("o", diff_w_out, 0)],
        2: [("ffn_out", ffn_w_out, 2), ("q", diff_w_q, 1), ("o", diff_w_out, 1)],
        3: [("ffn_out", ffn_w_out, 3)],
    }
    ready = {}

    k_shared = v_shared = None
    for l in range(DEPTH):
        if l < N_A_LAYERS:
            i = l
            w_in_b = gla_in_b[0][None] if i == 0 else ready["gla_in", i]
            w_out_b = gla_out_b[0][None] if i == 0 else ready["gla_out", i]
            w2 = jnp.pad(gla_w_alpha2[i], ((0, pad), (0, 0))).astype(BF16)
            proj, a_low = _norm_proj(xt, _vec(gla_attn_norm, i), w_in_b, 0, n=GLA_MAIN, bm=1024, bn=1024,
                                     out_dtype=F32, gate_rank=GLA_GATE_RANK, w_transposed=True)
            o = _gla_core(proj, a_low, w2, _vec(gla_b_alpha, i), _vec(gla_head_norm, i))
            xt, xn = _res_mm_norm(o, w_out_b, 0, xt, _vec(ffn_norm, l), bm=512)
        else:
            j = l - N_A_LAYERS
            lambda_init = 0.8 - 0.6 * math.exp(-0.3 * l)
            q_scale = DIFF_HEAD_DIM ** -0.5 * LOG2E
            if l == N_A_LAYERS:
                kvq = _kvq_proj(xt, _vec(kv_norm), _vec(diff_attn_norm, j), ready["kv", 0], ready["q", j],
                                _vec(k_norm), _vec(diff_q_norm, j), q_scale, bm=1024, bn=1024)
                k_shared, v_shared = (kvq, 0), (kvq, DIFF_HEADS)
                q = (kvq, 2 * DIFF_HEADS)
            else:
                q = (_norm_proj(xt, _vec(diff_attn_norm, j), ready["q", j], 0, n=DIFF_QK, bm=1024, bn=1024,
                                out_dtype=BF16, gn_gain=_vec(diff_q_norm, j), scale=q_scale), 0)
            o = _diff_attn_core(q, k_shared, v_shared, slopes, _vec(diff_lambda_q1, j), _vec(diff_lambda_k1, j),
                                _vec(diff_lambda_q2, j), _vec(diff_lambda_k2, j), _vec(diff_head_norm, j),
                                lambda_init)
            xt, xn = _res_mm_norm(o, ready["o", j], 0, xt, _vec(ffn_norm, l), bm=512)
        jobs = side_jobs[l]
        act, casted = _ffn_up(xn, ffn_w_in, l, [(w, idx) for _, w, idx in jobs], bm=1024, bn=512)
        for (name, _, idx), w_b in zip(jobs, casted):
            ready[name, idx] = w_b[None]
        xt = _res_mm(act, ready["ffn_out", l], 0, xt, bm=1024, bn=512)
    return xt.reshape(BATCH, SEQ, D_MODEL)
```

```python
import functools
import math

import jax
import jax.numpy as jnp
from jax import lax
from jax.experimental import pallas as pl
from jax.experimental.pallas import tpu as pltpu

D_MODEL = 2048
BATCH = 4
SEQ = 2048
DEPTH = 4
TOKENS = BATCH * SEQ
N_A_LAYERS = DEPTH // 2
N_B_LAYERS = DEPTH - N_A_LAYERS
EPS = 1e-6

GLA_HEADS = 4
GLA_DK = D_MODEL // 2
GLA_DV = D_MODEL
GLA_HK = GLA_DK // GLA_HEADS
GLA_HV = GLA_DV // GLA_HEADS
GLA_GATE_RANK = 16
GLA_GATE_TAU = 16.0
GLA_CHUNK = 64
GLA_MAIN = 2 * GLA_DK + 2 * GLA_DV

DIFF_HEAD_DIM = 128
DIFF_HEADS = D_MODEL // (2 * DIFF_HEAD_DIM)
DIFF_VDIM = 2 * DIFF_HEAD_DIM
DIFF_QK = DIFF_HEADS * 2 * DIFF_HEAD_DIM
DIFF_V = DIFF_HEADS * DIFF_VDIM

D_FF = -(-8 * D_MODEL // (3 * 256)) * 256

LANES = 128
MASK_VALUE = -1e30

BF16 = jnp.bfloat16
F32 = jnp.float32

VMEM_LIMIT = 56 * 1024 * 1024


def _params(n_axes):
    return pltpu.CompilerParams(dimension_semantics=("arbitrary",) * n_axes,
                                vmem_limit_bytes=VMEM_LIMIT)


def _silu(x):
    return x / (1.0 + jnp.exp(-x))


def _vec(param, idx=None):
    if idx is None:
        return param.reshape(1, 1, -1), 0
    return param.reshape(param.shape[0], 1, param.shape[1]), idx


def _vec_spec(vec):
    arr, idx = vec
    return pl.BlockSpec((None, 1, arr.shape[2]), lambda *_: (idx, 0, 0))


NORM_ROWS = 128


def _fill_normed(x_ref, g_ref, xn_ref):
    gain = g_ref[...]

    def body(i, carry):
        rows = pl.ds(pl.multiple_of(i * NORM_ROWS, NORM_ROWS), NORM_ROWS)
        xv = x_ref[rows, :]
        ms = jnp.mean(xv * xv, axis=-1, keepdims=True)
        xn_ref[rows, :] = ((xv * lax.rsqrt(ms + EPS)) * gain).astype(BF16)
        return carry

    lax.fori_loop(0, x_ref.shape[0] // NORM_ROWS, body, 0, unroll=2)


def _norm_rows(x_ref, g_ref, xn_ref, rows):
    gain = g_ref[...]
    for start in range(rows.start, rows.stop, NORM_ROWS):
        chunk = slice(start, start + NORM_ROWS)
        xv = x_ref[chunk, :]
        ms = jnp.mean(xv * xv, axis=-1, keepdims=True)
        xn_ref[chunk, :] = ((xv * lax.rsqrt(ms + EPS)) * gain).astype(BF16)


def _group_norm_store(y, gn_ref, o_ref, scale):
    gain = gn_ref[...]
    for g in range(y.shape[1] // LANES):
        yg = y[:, g * LANES:(g + 1) * LANES]
        ms = jnp.mean(yg * yg, axis=-1, keepdims=True)
        val = (yg * lax.rsqrt(ms + EPS)) * gain
        if scale != 1.0:
            val = val * scale
        o_ref[:, g * LANES:(g + 1) * LANES] = val.astype(o_ref.dtype)


NT_DIMS = (((1,), (1,)), ((), ()))


def _norm_proj_kernel(x_ref, g_ref, w_ref, *rest, group_norm, scale, gate_rank, w_transposed):
    rest = list(rest)
    w1_ref = rest.pop(0) if gate_rank else None
    gn_ref = rest.pop(0) if group_norm else None
    o_ref = rest.pop(0)
    a_ref = rest.pop(0) if gate_rank else None
    xn_ref = rest.pop(0)

    bm = x_ref.shape[0]
    j = pl.program_id(1)

    def project(w_blk, rows):
        if w_transposed:
            return lax.dot_general(xn_ref[rows, :], w_blk, NT_DIMS, preferred_element_type=F32)
        return jnp.dot(xn_ref[rows, :], w_blk, preferred_element_type=F32)

    def emit(rows):
        y = project(w_ref[...], rows)
        if group_norm:
            _group_norm_store(y, gn_ref, o_ref.at[rows, :], scale)
        else:
            o_ref[rows, :] = y.astype(o_ref.dtype)

    @pl.when(j == 0)
    def _():
        for rows in (slice(0, bm // 2), slice(bm // 2, bm)):
            _norm_rows(x_ref, g_ref, xn_ref, rows)
            emit(rows)
        if gate_rank:
            a_ref[:, :gate_rank] = project(w1_ref[...], slice(0, bm))
            a_ref[:, gate_rank:] = jnp.zeros((bm, a_ref.shape[1] - gate_rank), F32)

    @pl.when(j > 0)
    def _():
        emit(slice(0, bm))


def _w_spec(k, bn, layer, col_block0):
    return pl.BlockSpec((None, k, bn), lambda i, j: (layer, 0, j + col_block0))


def _norm_proj(x, gain, w, layer, *, n, bm, bn, out_dtype, col0=0, gate_rank=0, gn_gain=None, scale=1.0,
               w_transposed=False):
    m, k = x.shape
    group_norm = gn_gain is not None
    assert not gate_rank or w_transposed
    if w_transposed:
        w_spec = pl.BlockSpec((None, bn, k), lambda i, j: (layer, j + col0 // bn, 0))
    else:
        w_spec = _w_spec(k, bn, layer, col0 // bn)
    in_specs = [pl.BlockSpec((bm, k), lambda i, j: (i, 0)), _vec_spec(gain), w_spec]
    args = [x, gain[0], w]
    if gate_rank:
        in_specs.append(pl.BlockSpec((None, gate_rank, k), lambda i, j: (layer, (col0 + n) // gate_rank, 0)))
        args.append(w)
    if group_norm:
        in_specs.append(_vec_spec(gn_gain))
        args.append(gn_gain[0])
    out_shape = [jax.ShapeDtypeStruct((m, n), out_dtype)]
    out_specs = [pl.BlockSpec((bm, bn), lambda i, j: (i, j))]
    if gate_rank:
        out_shape.append(jax.ShapeDtypeStruct((m, LANES), F32))
        out_specs.append(pl.BlockSpec((bm, LANES), lambda i, j: (i, 0)))
    res = pl.pallas_call(
        functools.partial(_norm_proj_kernel, group_norm=group_norm, scale=scale, gate_rank=gate_rank,
                          w_transposed=w_transposed),
        grid=(m // bm, n // bn),
        in_specs=in_specs,
        out_specs=out_specs,
        out_shape=out_shape,
        scratch_shapes=[pltpu.VMEM((bm, k), BF16)],
        compiler_params=_params(2),
        name="norm_proj",
    )(*args)
    return res if gate_rank else res[0]


def _kvq_kernel(x_ref, gkv_ref, gq_ref, wkv_ref, wq_ref, kn_ref, qn_ref, o_ref, xkv_ref, xq_ref, *,
                k_blocks, kv_blocks, q_scale):
    j = pl.program_id(1)
    bm = x_ref.shape[0]

    @pl.when(j == 0)
    def _():
        for rows in (slice(0, bm // 2), slice(bm // 2, bm)):
            _norm_rows(x_ref, gkv_ref, xkv_ref, rows)
            _norm_rows(x_ref, gq_ref, xq_ref, rows)
            _group_norm_store(jnp.dot(xkv_ref[rows, :], wkv_ref[...], preferred_element_type=F32), kn_ref,
                              o_ref.at[rows, :], 1.0)

    @pl.when(jnp.logical_and(j > 0, j < k_blocks))
    def _():
        _group_norm_store(jnp.dot(xkv_ref[...], wkv_ref[...], preferred_element_type=F32), kn_ref, o_ref, 1.0)

    @pl.when(jnp.logical_and(j >= k_blocks, j < kv_blocks))
    def _():
        o_ref[...] = jnp.dot(xkv_ref[...], wkv_ref[...], preferred_element_type=F32).astype(o_ref.dtype)

    @pl.when(j >= kv_blocks)
    def _():
        _group_norm_store(jnp.dot(xq_ref[...], wq_ref[...], preferred_element_type=F32), qn_ref, o_ref, q_scale)


def _kvq_proj(x, kv_gain, q_gain, w_kv, w_q, k_norm, q_norm, q_scale, *, bm, bn):
    m, k = x.shape
    kv_blocks, q_blocks = w_kv.shape[2] // bn, w_q.shape[2] // bn
    return pl.pallas_call(
        functools.partial(_kvq_kernel, k_blocks=DIFF_QK // bn, kv_blocks=kv_blocks, q_scale=q_scale),
        grid=(m // bm, kv_blocks + q_blocks),
        in_specs=[pl.BlockSpec((bm, k), lambda i, j: (i, 0)), _vec_spec(kv_gain), _vec_spec(q_gain),
                  pl.BlockSpec((None, k, bn), lambda i, j: (0, 0, jnp.minimum(j, kv_blocks - 1))),
                  pl.BlockSpec((None, k, bn), lambda i, j: (0, 0, jnp.maximum(j - kv_blocks, 0))),
                  _vec_spec(k_norm), _vec_spec(q_norm)],
        out_specs=pl.BlockSpec((bm, bn), lambda i, j: (i, j)),
        out_shape=jax.ShapeDtypeStruct((m, (kv_blocks + q_blocks) * bn), BF16),
        scratch_shapes=[pltpu.VMEM((bm, k), BF16), pltpu.VMEM((bm, k), BF16)],
        compiler_params=_params(2),
        name="kvq_proj",
    )(x, kv_gain[0], q_gain[0], w_kv, w_q, k_norm[0], q_norm[0])


def _ffn_up_kernel(xn_ref, wg_ref, wu_ref, *rest, n_side):
    side_src = rest[:n_side]
    o_ref = rest[n_side]
    side_dst = rest[n_side + 1:2 * n_side + 1]
    wgb_ref, wub_ref = rest[2 * n_side + 1:]

    def step(round_weights):
        xn = xn_ref[...]
        half = o_ref.shape[1] // 2
        for cols in (slice(0, half), slice(half, 2 * half)):
            if round_weights:
                wgb_ref[:, cols] = wg_ref[:, cols].astype(BF16)
                wub_ref[:, cols] = wu_ref[:, cols].astype(BF16)
            g = jnp.dot(xn, wgb_ref[:, cols], preferred_element_type=F32)
            u = jnp.dot(xn, wub_ref[:, cols], preferred_element_type=F32)
            o_ref[:, cols] = (_silu(g) * u).astype(o_ref.dtype)

        for src, dst in zip(side_src, side_dst):
            dst[...] = src[...].astype(dst.dtype)

    new_tile = pl.program_id(1) == 0
    pl.when(new_tile)(functools.partial(step, True))
    pl.when(jnp.logical_not(new_tile))(functools.partial(step, False))


def _ffn_up(xn, w_in, layer, side, *, bm, bn):
    m, k = xn.shape
    nblk, mblk = D_FF // bn, m // bm
    n_steps = nblk * mblk
    side_in, side_out, side_shapes = [], [], []
    for w, w_layer in side:
        _, r, c = w.shape
        rows = min(t for t in range(16, r + 1, 16) if r % t == 0 and r // t <= n_steps)
        last = r // rows - 1
        side_in.append(pl.BlockSpec(
            (None, rows, c), lambda j, i, w_layer=w_layer, last=last: (w_layer, jnp.minimum(j * mblk + i, last), 0)))
        side_out.append(pl.BlockSpec((rows, c), lambda j, i, last=last: (jnp.minimum(j * mblk + i, last), 0)))
        side_shapes.append(jax.ShapeDtypeStruct((r, c), BF16))
    res = pl.pallas_call(
        functools.partial(_ffn_up_kernel, n_side=len(side)),
        grid=(nblk, mblk),
        in_specs=[pl.BlockSpec((bm, k), lambda j, i: (i, 0)),
                  pl.BlockSpec((None, k, bn), lambda j, i: (layer, 0, j)),
                  pl.BlockSpec((None, k, bn), lambda j, i: (layer, 0, j + nblk))] + side_in,
        out_specs=[pl.BlockSpec((bm, bn), lambda j, i: (i, j))] + side_out,
        out_shape=[jax.ShapeDtypeStruct((m, D_FF), BF16)] + side_shapes,
        scratch_shapes=[pltpu.VMEM((k, bn), BF16), pltpu.VMEM((k, bn), BF16)],
        compiler_params=_params(2),
        name="ffn_up",
    )(xn, w_in, w_in, *[w for w, _ in side])
    return res[0], res[1:]


def _res_mm_norm_kernel(a_ref, w_ref, r_ref, g_ref, o_ref, xn_ref):
    o_ref[...] = r_ref[...] + jnp.dot(a_ref[...], w_ref[...], preferred_element_type=F32)
    _fill_normed(o_ref, g_ref, xn_ref)


def _res_mm_norm(a, w, layer, res, gain, *, bm):
    m, k = a.shape
    n = w.shape[2]
    row_blk = lambda i: (i, 0)
    return pl.pallas_call(
        _res_mm_norm_kernel,
        grid=(m // bm,),
        in_specs=[pl.BlockSpec((bm, k), row_blk),
                  pl.BlockSpec((None, k, n), lambda i: (layer, 0, 0)),
                  pl.BlockSpec((bm, n), row_blk),
                  _vec_spec(gain)],
        out_specs=[pl.BlockSpec((bm, n), row_blk), pl.BlockSpec((bm, n), row_blk)],
        out_shape=[jax.ShapeDtypeStruct((m, n), F32), jax.ShapeDtypeStruct((m, n), BF16)],
        compiler_params=_params(1),
        name="res_mm_norm",
    )(a, w, res, gain[0])


def _res_mm_kernel(a_ref, w_ref, r_ref, o_ref):
    o_ref[...] = r_ref[...] + jnp.dot(a_ref[...], w_ref[...], preferred_element_type=F32)


def _res_mm(a, w, layer, res, *, bm, bn):
    m, k = a.shape
    n = w.shape[2]
    return pl.pallas_call(
        _res_mm_kernel,
        grid=(m // bm, n // bn),
        in_specs=[pl.BlockSpec((bm, k), lambda i, j: (i, 0)),
                  _w_spec(k, bn, layer, 0),
                  pl.BlockSpec((bm, bn), lambda i, j: (i, j))],
        out_specs=pl.BlockSpec((bm, bn), lambda i, j: (i, j)),
        out_shape=jax.ShapeDtypeStruct((m, n), F32),
        compiler_params=_params(2),
        name="res_mm",
    )(a, w, res)


GLA_ROWS = 512


def _split2(x):
    hi = x.astype(BF16)
    lo = (x - hi.astype(F32)).astype(BF16)
    return hi, lo


def _gla_kernel(p_ref, a_ref, w2_ref, b_ref, hg_ref, o_ref, state_ref):
    c = GLA_CHUNK

    @pl.when(pl.program_id(1) == 0)
    def _():
        state_ref[...] = jnp.zeros_like(state_ref)

    row = lax.broadcasted_iota(jnp.int32, (c, c), 0)
    col = lax.broadcasted_iota(jnp.int32, (c, c), 1)
    causal = row >= col
    tril = causal.astype(BF16)
    head_gain = hg_ref[...]
    tn_dims = (((0,), (0,)), ((), ()))
    nt_dims = (((1,), (1,)), ((), ()))

    heads = range(GLA_HEADS)
    hk = [slice(h * GLA_HK, (h + 1) * GLA_HK) for h in heads]
    hv = [slice(h * GLA_HV, (h + 1) * GLA_HV) for h in heads]

    def rows_of(ci):
        return slice(ci * c, (ci + 1) * c)

    def gate_preact(ci):
        a_low = a_ref[rows_of(ci), :].astype(BF16)
        return jnp.dot(a_low, w2_ref[...], preferred_element_type=F32) + b_ref[...]

    def cum_log_decay(z):
        log_a = (jnp.minimum(z, 0.0) - jnp.log(1.0 + jnp.exp(-jnp.abs(z)))) / GLA_GATE_TAU
        return sum(jnp.dot(tril, part, preferred_element_type=F32) for part in _split2(log_a))

    def decayed_operands(ci, bcum):
        b_last = bcum[c - 1:c, :]
        q = p_ref[rows_of(ci), 0:GLA_DK]
        k = p_ref[rows_of(ci), GLA_DK:2 * GLA_DK]
        q_dec = ((q * (GLA_HK ** -0.5)) * jnp.exp(bcum)).astype(BF16)
        k_inv = (k * jnp.exp(-bcum)).astype(BF16)
        k_end = (k * jnp.exp(b_last - bcum)).astype(BF16)
        return q_dec, k_inv, k_end, jnp.exp(b_last)

    n_chunks = GLA_ROWS // c
    prepared = decayed_operands(0, cum_log_decay(gate_preact(0)))
    for ci in range(n_chunks):
        rows = rows_of(ci)
        q_dec, k_inv, k_end, chunk_decay = prepared
        has_next = ci + 1 < n_chunks
        if has_next:
            z_next = gate_preact(ci + 1)
        v_b = p_ref[rows, 2 * GLA_DK:2 * GLA_DK + GLA_DV].astype(BF16)
        states = [state_ref[h] for h in heads]
        o_inter = [jnp.dot(q_dec[:, hk[h]], states[h].astype(BF16), preferred_element_type=F32) for h in heads]
        kv = [lax.dot_general(k_end[:, hk[h]], v_b[:, hv[h]], tn_dims, preferred_element_type=F32)
              for h in heads]
        attn = [lax.dot_general(q_dec[:, hk[h]], k_inv[:, hk[h]], nt_dims, preferred_element_type=F32)
                for h in heads]
        if has_next:
            bcum_next = cum_log_decay(z_next)
        o_intra = [jnp.dot(jnp.where(causal, attn[h], 0.0).astype(BF16), v_b[:, hv[h]],
                           preferred_element_type=F32) for h in heads]
        if has_next:
            prepared = decayed_operands(ci + 1, bcum_next)
        for h in heads:
            decay = jnp.broadcast_to(chunk_decay[:, hk[h]], (LANES, GLA_HK)).T
            state_ref[h] = states[h] * jnp.concatenate([decay] * (GLA_HV // LANES), axis=1) + kv[h]
            o = o_intra[h] + o_inter[h]
            ms = jnp.mean(o * o, axis=-1, keepdims=True)
            o = (o * lax.rsqrt(ms + EPS)) * head_gain
            r = p_ref[rows, 2 * GLA_DK + GLA_DV + h * GLA_HV:2 * GLA_DK + GLA_DV + (h + 1) * GLA_HV]
            o_ref[rows, hv[h]] = (o * _silu(r)).astype(o_ref.dtype)


def _gla_core(proj, a_low, w2, b_alpha, head_gain):
    nblk = SEQ // GLA_ROWS
    return pl.pallas_call(
        _gla_kernel,
        grid=(BATCH, nblk),
        in_specs=[pl.BlockSpec((GLA_ROWS, GLA_MAIN), lambda b, n: (b * nblk + n, 0)),
                  pl.BlockSpec((GLA_ROWS, LANES), lambda b, n: (b * nblk + n, 0)),
                  pl.BlockSpec((LANES, GLA_DK), lambda b, n: (0, 0)),
                  _vec_spec(b_alpha), _vec_spec(head_gain)],
        out_specs=pl.BlockSpec((GLA_ROWS, GLA_DV), lambda b, n: (b * nblk + n, 0)),
        out_shape=jax.ShapeDtypeStruct((TOKENS, GLA_DV), BF16),
        scratch_shapes=[pltpu.VMEM((GLA_HEADS, GLA_HK, GLA_HV), F32)],
        compiler_params=_params(2),
        name="gla_core",
    )(proj, a_low, w2, b_alpha[0], head_gain[0])


ATT_TQ = 256
LOG2E = math.log2(math.e)


def _lane_tiles(x):
    return [x[:, i * LANES:(i + 1) * LANES] for i in range(x.shape[1] // LANES)]


def _attn_kernel(slope_ref, q_ref, k_ref, v_ref, lq1_ref, lk1_ref, lq2_ref, lk2_ref, hg_ref, o_ref,
                 qx_ref, kx_ref, *, lambda_init):
    tq, dh = ATT_TQ, DIFF_HEAD_DIM
    nt_dims = (((1,), (1,)), ((), ()))
    lam = (jnp.exp(jnp.sum(lq1_ref[...] * lk1_ref[...], axis=-1, keepdims=True))
           - jnp.exp(jnp.sum(lq2_ref[...] * lk2_ref[...], axis=-1, keepdims=True))
           + lambda_init)
    out_gain = hg_ref[...] * (1.0 - lambda_init)

    slope2 = slope_ref[0] * LOG2E
    s_hi = slope2.astype(BF16).astype(F32)
    s_mid = (slope2 - s_hi).astype(BF16).astype(F32)
    s_lo = (slope2 - s_hi - s_mid).astype(BF16).astype(F32)
    lane = lax.broadcasted_iota(jnp.int32, (1, LANES), 1)
    q_feat = jnp.where(lane < 2, s_hi, jnp.where(lane < 4, s_mid, jnp.where(lane < 6, s_lo, 0.0)))
    pos = lax.broadcasted_iota(jnp.int32, (SEQ, LANES), 0)
    lane_k = lax.broadcasted_iota(jnp.int32, (SEQ, LANES), 1)
    pos_lo = jnp.bitwise_and(pos, 255)
    pos_piece = jnp.where(jnp.bitwise_and(lane_k, 1) == 0, pos - pos_lo, pos_lo)
    k_feat = jnp.where(lane_k < 6, pos_piece, 0).astype(F32).astype(BF16)
    q_feat = jnp.broadcast_to(q_feat, (SEQ, LANES)).astype(BF16)
    for comp in range(2):
        qx_ref[comp, :, 0:dh] = q_ref[:, comp * dh:(comp + 1) * dh]
        qx_ref[comp, :, dh:2 * dh] = q_feat
        kx_ref[comp, :, 0:dh] = k_ref[:, comp * dh:(comp + 1) * dh]
        kx_ref[comp, :, dh:2 * dh] = k_feat

    row = lax.broadcasted_iota(jnp.int32, (tq, tq), 0)
    col = lax.broadcasted_iota(jnp.int32, (tq, tq), 1)
    causal_tiles = _lane_tiles(row >= col)

    def scores(qi):
        q0, n_keys = qi * tq, (qi + 1) * tq
        return [lax.dot_general(qx_ref[comp, q0:q0 + tq, :], kx_ref[comp, 0:n_keys, :],
                                nt_dims, preferred_element_type=F32) for comp in range(2)]

    def attend(qi, qk):
        q0, n_keys = qi * tq, (qi + 1) * tq
        v_blk = v_ref[0:n_keys, :]
        n_diag = len(causal_tiles)
        outs = []
        for comp in range(2):
            tiles = _lane_tiles(qk[comp])
            tiles = tiles[:-n_diag] + [jnp.where(keep, ti, MASK_VALUE)
                                       for keep, ti in zip(causal_tiles, tiles[-n_diag:])]
            m = jnp.max(functools.reduce(jnp.maximum, tiles), axis=-1, keepdims=True)
            p_tiles = [jnp.exp2(ti - m) for ti in tiles]
            l = jnp.sum(functools.reduce(jnp.add, p_tiles), axis=-1, keepdims=True)
            p = jnp.concatenate([pt.astype(BF16) for pt in p_tiles], axis=1)
            outs.append(jnp.dot(p, v_blk, preferred_element_type=F32) / l)
        o = outs[0] - lam * outs[1]
        ms = jnp.mean(o * o, axis=-1, keepdims=True)
        o = (o * lax.rsqrt(ms + EPS)) * out_gain
        o_ref[q0:q0 + tq, :] = o.astype(o_ref.dtype)

    order = list(reversed(range(SEQ // tq)))
    qk_next = scores(order[0])
    for n, qi in enumerate(order):
        qk = qk_next
        if n + 1 < len(order):
            qk_next = scores(order[n + 1])
        attend(qi, qk)


def _diff_attn_core(q, k, v, slopes, lq1, lk1, lq2, lk2, head_gain, lambda_init):
    def head_blk(first):
        return pl.BlockSpec((SEQ, DIFF_VDIM), lambda b, h: (b, first + h))

    (q, q_first), (k, k_first), (v, v_first) = q, k, v
    vecs = [lq1, lk1, lq2, lk2, head_gain]
    return pl.pallas_call(
        functools.partial(_attn_kernel, lambda_init=lambda_init),
        grid=(BATCH, DIFF_HEADS),
        in_specs=[pl.BlockSpec((1, 1, LANES), lambda b, h: (h, 0, 0)),
                  head_blk(q_first), head_blk(k_first), head_blk(v_first)] + [_vec_spec(p) for p in vecs],
        out_specs=head_blk(0),
        out_shape=jax.ShapeDtypeStruct((TOKENS, DIFF_V), BF16),
        scratch_shapes=[pltpu.VMEM((2, SEQ, 2 * DIFF_HEAD_DIM), BF16),
                        pltpu.VMEM((2, SEQ, 2 * DIFF_HEAD_DIM), BF16)],
        compiler_params=_params(2),
        name="diff_attn",
    )(slopes, q, k, v, *[p[0] for p in vecs])


CAST_BLOCK_BYTES = 8 * 1024 * 1024


def _cast_kernel(x_ref, o_ref):
    o_ref[...] = x_ref[...].astype(o_ref.dtype)


def _to_bf16(w, layer):
    _, r, c = w.shape
    bc = next((t for t in (2048, 1024) if c % t == 0), c)
    br = max(t for t in range(16, r + 1, 16) if r % t == 0 and t * bc * 4 <= CAST_BLOCK_BYTES)
    return pl.pallas_call(
        _cast_kernel,
        grid=(r // br, c // bc),
        in_specs=[pl.BlockSpec((None, br, bc), lambda i, j: (layer, i, j))],
        out_specs=pl.BlockSpec((br, bc), lambda i, j: (i, j)),
        out_shape=jax.ShapeDtypeStruct((r, c), BF16),
        compiler_params=_params(2),
        name="cast_bf16",
    )(w)


def kernel(x, gla_attn_norm, gla_w_in, gla_w_alpha2, gla_b_alpha, gla_head_norm, gla_w_out, kv_norm, w_kv, k_norm, diff_attn_norm, diff_w_q, diff_q_norm, diff_lambda_q1, diff_lambda_k1, diff_lambda_q2, diff_lambda_k2, diff_head_norm, diff_w_out, ffn_norm, ffn_w_in, ffn_w_out):
    xt = x.reshape(TOKENS, D_MODEL)
    pad = LANES - GLA_GATE_RANK
    slopes = (2.0 ** (-8.0 * jnp.arange(1, DIFF_HEADS + 1, dtype=F32) / DIFF_HEADS))
    slopes = jnp.broadcast_to(slopes[:, None, None], (DIFF_HEADS, 1, LANES))

    gla_in_t = jnp.swapaxes(gla_w_in, 1, 2)
    w_kv3 = w_kv[None]

    gla_in_b = {0: _to_bf16(gla_in_t, 0)}
    gla_out_b = {0: _to_bf16(gla_w_out, 0)}
    side_jobs = {
        0: [("ffn_out", ffn_w_out, 0), ("gla_in", gla_in_t, 1), ("gla_out", gla_w_out, 1)],
        1: [("ffn_out", ffn_w_out, 1), ("kv", w_kv3, 0), ("q", diff_w_q, 0), ("o", diff_w_out, 0)],
        2: [("ffn_out", ffn_w_out, 2), ("q", diff_w_q, 1), ("o", diff_w_out, 1)],
        3: [("ffn_out", ffn_w_out, 3)],
    }
    ready = {}

    k_shared = v_shared = None
    for l in range(DEPTH):
        if l < N_A_LAYERS:
            i = l
            w_in_b = gla_in_b[0][None] if i == 0 else ready["gla_in", i]
            w_out_b = gla_out_b[0][None] if i == 0 else ready["gla_out", i]
            w2 = jnp.pad(gla_w_alpha2[i], ((0, pad), (0, 0))).astype(BF16)
            proj, a_low = _norm_proj(xt, _vec(gla_attn_norm, i), w_in_b, 0, n=GLA_MAIN, bm=1024, bn=1024,
                                     out_dtype=F32, gate_rank=GLA_GATE_RANK, w_transposed=True)
            o = _gla_core(proj, a_low, w2, _vec(gla_b_alpha, i), _vec(gla_head_norm, i))
            xt, xn = _res_mm_norm(o, w_out_b, 0, xt, _vec(ffn_norm, l), bm=512)
        else:
            j = l - N_A_LAYERS
            lambda_init = 0.8 - 0.6 * math.exp(-0.3 * l)
            q_scale = DIFF_HEAD_DIM ** -0.5 * LOG2E
            if l == N_A_LAYERS:
                kvq = _kvq_proj(xt, _vec(kv_norm), _vec(diff_attn_norm, j), ready["kv", 0], ready["q", j],
                                _vec(k_norm), _vec(diff_q_norm, j), q_scale, bm=1024, bn=1024)
                k_shared, v_shared = (kvq, 0), (kvq, DIFF_HEADS)
                q = (kvq, 2 * DIFF_HEADS)
            else:
                q = (_norm_proj(xt, _vec(diff_attn_norm, j), ready["q", j], 0, n=DIFF_QK, bm=1024, bn=1024,
                                out_dtype=BF16, gn_gain=_vec(diff_q_norm, j), scale=q_scale), 0)
            o = _diff_attn_core(q, k_shared, v_shared, slopes, _vec(diff_lambda_q1, j), _vec(diff_lambda_k1, j),
                                _vec(diff_lambda_q2, j), _vec(diff_lambda_k2, j), _vec(diff_head_norm, j),
                                lambda_init)
            xt, xn = _res_mm_norm(o, ready["o", j], 0, xt, _vec(ffn_norm, l), bm=512)
        jobs = side_jobs[l]
        act, casted = _ffn_up(xn, ffn_w_in, l, [(w, idx) for _, w, idx in jobs], bm=1024, bn=512)
        for (name, _, idx), w_b in zip(jobs, casted):
            ready[name, idx] = w_b[None]
        xt = _res_mm(act, ready["ffn_out", l], 0, xt, bm=1024, bn=512)
    return xt.reshape(BATCH, SEQ, D_MODEL)
```

```python
import functools
import math

import jax
import jax.numpy as jnp
from jax import lax
from jax.experimental import pallas as pl
from jax.experimental.pallas import tpu as pltpu

D_MODEL = 2048
BATCH = 4
SEQ = 2048
DEPTH = 4
TOKENS = BATCH * SEQ
N_A_LAYERS = DEPTH // 2
N_B_LAYERS = DEPTH - N_A_LAYERS
EPS = 1e-6

GLA_HEADS = 4
GLA_DK = D_MODEL // 2
GLA_DV = D_MODEL
GLA_HK = GLA_DK // GLA_HEADS
GLA_HV = GLA_DV // GLA_HEADS
GLA_GATE_RANK = 16
GLA_GATE_TAU = 16.0
GLA_CHUNK = 64
GLA_MAIN = 2 * GLA_DK + 2 * GLA_DV

DIFF_HEAD_DIM = 128
DIFF_HEADS = D_MODEL // (2 * DIFF_HEAD_DIM)
DIFF_VDIM = 2 * DIFF_HEAD_DIM
DIFF_QK = DIFF_HEADS * 2 * DIFF_HEAD_DIM
DIFF_V = DIFF_HEADS * DIFF_VDIM

D_FF = -(-8 * D_MODEL // (3 * 256)) * 256

LANES = 128
MASK_VALUE = -1e30

BF16 = jnp.bfloat16
F32 = jnp.float32

VMEM_LIMIT = 56 * 1024 * 1024


def _params(n_axes):
    return pltpu.CompilerParams(dimension_semantics=("arbitrary",) * n_axes,
                                vmem_limit_bytes=VMEM_LIMIT)


def _silu(x):
    return x / (1.0 + jnp.exp(-x))


def _vec(param, idx=None):
    if idx is None:
        return param.reshape(1, 1, -1), 0
    return param.reshape(param.shape[0], 1, param.shape[1]), idx


def _vec_spec(vec):
    arr, idx = vec
    return pl.BlockSpec((None, 1, arr.shape[2]), lambda *_: (idx, 0, 0))


NORM_ROWS = 128


def _norm_rows(x_ref, g_ref, xn_ref, rows):
    gain = g_ref[...]
    for start in range(rows.start, rows.stop, NORM_ROWS):
        chunk = slice(start, start + NORM_ROWS)
        xv = x_ref[chunk, :]
        ms = jnp.mean(xv * xv, axis=-1, keepdims=True)
        xn_ref[chunk, :] = ((xv * lax.rsqrt(ms + EPS)) * gain).astype(BF16)


def _group_norm_store(y, gn_ref, o_ref, scale):
    gain = gn_ref[...]
    for g in range(y.shape[1] // LANES):
        yg = y[:, g * LANES:(g + 1) * LANES]
        ms = jnp.mean(yg * yg, axis=-1, keepdims=True)
        val = (yg * lax.rsqrt(ms + EPS)) * gain
        if scale != 1.0:
            val = val * scale
        o_ref[:, g * LANES:(g + 1) * LANES] = val.astype(o_ref.dtype)


NT_DIMS = (((1,), (1,)), ((), ()))


def _norm_proj_kernel(x_ref, g_ref, w_ref, *rest, group_norm, scale, gate_rank, w_transposed):
    rest = list(rest)
    w1_ref = rest.pop(0) if gate_rank else None
    gn_ref = rest.pop(0) if group_norm else None
    o_ref = rest.pop(0)
    a_ref = rest.pop(0) if gate_rank else None
    xn_ref = rest.pop(0)

    bm = x_ref.shape[0]
    j = pl.program_id(1)

    def project(w_blk, rows):
        if w_transposed:
            return lax.dot_general(xn_ref[rows, :], w_blk, NT_DIMS, preferred_element_type=F32)
        return jnp.dot(xn_ref[rows, :], w_blk, preferred_element_type=F32)

    def emit(rows):
        y = project(w_ref[...], rows)
        if group_norm:
            _group_norm_store(y, gn_ref, o_ref.at[rows, :], scale)
        else:
            o_ref[rows, :] = y.astype(o_ref.dtype)

    @pl.when(j == 0)
    def _():
        for rows in (slice(0, bm // 2), slice(bm // 2, bm)):
            _norm_rows(x_ref, g_ref, xn_ref, rows)
            emit(rows)
        if gate_rank:
            a_ref[:, :gate_rank] = project(w1_ref[...], slice(0, bm))
            a_ref[:, gate_rank:] = jnp.zeros((bm, a_ref.shape[1] - gate_rank), F32)

    @pl.when(j > 0)
    def _():
        emit(slice(0, bm))


def _w_spec(k, bn, layer, col_block0):
    return pl.BlockSpec((None, k, bn), lambda i, j: (layer, 0, j + col_block0))


def _norm_proj(x, gain, w, layer, *, n, bm, bn, out_dtype, col0=0, gate_rank=0, gn_gain=None, scale=1.0,
               w_transposed=False):
    m, k = x.shape
    group_norm = gn_gain is not None
    assert not gate_rank or w_transposed
    if w_transposed:
        w_spec = pl.BlockSpec((None, bn, k), lambda i, j: (layer, j + col0 // bn, 0))
    else:
        w_spec = _w_spec(k, bn, layer, col0 // bn)
    in_specs = [pl.BlockSpec((bm, k), lambda i, j: (i, 0)), _vec_spec(gain), w_spec]
    args = [x, gain[0], w]
    if gate_rank:
        in_specs.append(pl.BlockSpec((None, gate_rank, k), lambda i, j: (layer, (col0 + n) // gate_rank, 0)))
        args.append(w)
    if group_norm:
        in_specs.append(_vec_spec(gn_gain))
        args.append(gn_gain[0])
    out_shape = [jax.ShapeDtypeStruct((m, n), out_dtype)]
    out_specs = [pl.BlockSpec((bm, bn), lambda i, j: (i, j))]
    if gate_rank:
        out_shape.append(jax.ShapeDtypeStruct((m, LANES), F32))
        out_specs.append(pl.BlockSpec((bm, LANES), lambda i, j: (i, 0)))
    res = pl.pallas_call(
        functools.partial(_norm_proj_kernel, group_norm=group_norm, scale=scale, gate_rank=gate_rank,
                          w_transposed=w_transposed),
        grid=(m // bm, n // bn),
        in_specs=in_specs,
        out_specs=out_specs,
        out_shape=out_shape,
        scratch_shapes=[pltpu.VMEM((bm, k), BF16)],
        compiler_params=_params(2),
        name="norm_proj",
    )(*args)
    return res if gate_rank else res[0]


def _kvq_kernel(x_ref, gkv_ref, gq_ref, wkv_ref, wq_ref, kn_ref, qn_ref, o_ref, xkv_ref, xq_ref, *,
                k_blocks, kv_blocks, q_scale):
    j = pl.program_id(1)
    bm = x_ref.shape[0]

    @pl.when(j == 0)
    def _():
        for rows in (slice(0, bm // 2), slice(bm // 2, bm)):
            _norm_rows(x_ref, gkv_ref, xkv_ref, rows)
            _norm_rows(x_ref, gq_ref, xq_ref, rows)
            _group_norm_store(jnp.dot(xkv_ref[rows, :], wkv_ref[...], preferred_element_type=F32), kn_ref,
                              o_ref.at[rows, :], 1.0)

    @pl.when(jnp.logical_and(j > 0, j < k_blocks))
    def _():
        _group_norm_store(jnp.dot(xkv_ref[...], wkv_ref[...], preferred_element_type=F32), kn_ref, o_ref, 1.0)

    @pl.when(jnp.logical_and(j >= k_blocks, j < kv_blocks))
    def _():
        o_ref[...] = jnp.dot(xkv_ref[...], wkv_ref[...], preferred_element_type=F32).astype(o_ref.dtype)

    @pl.when(j >= kv_blocks)
    def _():
        _group_norm_store(jnp.dot(xq_ref[...], wq_ref[...], preferred_element_type=F32), qn_ref, o_ref, q_scale)


def _kvq_proj(x, kv_gain, q_gain, w_kv, w_q, k_norm, q_norm, q_scale, *, bm, bn):
    m, k = x.shape
    kv_blocks, q_blocks = w_kv.shape[2] // bn, w_q.shape[2] // bn
    return pl.pallas_call(
        functools.partial(_kvq_kernel, k_blocks=DIFF_QK // bn, kv_blocks=kv_blocks, q_scale=q_scale),
        grid=(m // bm, kv_blocks + q_blocks),
        in_specs=[pl.BlockSpec((bm, k), lambda i, j: (i, 0)), _vec_spec(kv_gain), _vec_spec(q_gain),
                  pl.BlockSpec((None, k, bn), lambda i, j: (0, 0, jnp.minimum(j, kv_blocks - 1))),
                  pl.BlockSpec((None, k, bn), lambda i, j: (0, 0, jnp.maximum(j - kv_blocks, 0))),
                  _vec_spec(k_norm), _vec_spec(q_norm)],
        out_specs=pl.BlockSpec((bm, bn), lambda i, j: (i, j)),
        out_shape=jax.ShapeDtypeStruct((m, (kv_blocks + q_blocks) * bn), BF16),
        scratch_shapes=[pltpu.VMEM((bm, k), BF16), pltpu.VMEM((bm, k), BF16)],
        compiler_params=_params(2),
        name="kvq_proj",
    )(x, kv_gain[0], q_gain[0], w_kv, w_q, k_norm[0], q_norm[0])


def _ffn_up_kernel(xn_ref, wg_ref, wu_ref, *rest, n_side):
    side_src = rest[:n_side]
    o_ref = rest[n_side]
    side_dst = rest[n_side + 1:2 * n_side + 1]
    wgb_ref, wub_ref = rest[2 * n_side + 1:]

    def step(round_weights):
        xn = xn_ref[...]
        half = o_ref.shape[1] // 2
        for cols in (slice(0, half), slice(half, 2 * half)):
            if round_weights:
                wgb_ref[:, cols] = wg_ref[:, cols].astype(BF16)
                wub_ref[:, cols] = wu_ref[:, cols].astype(BF16)
            g = jnp.dot(xn, wgb_ref[:, cols], preferred_element_type=F32)
            u = jnp.dot(xn, wub_ref[:, cols], preferred_element_type=F32)
            o_ref[:, cols] = (_silu(g) * u).astype(o_ref.dtype)

        for src, dst in zip(side_src, side_dst):
            dst[...] = src[...].astype(dst.dtype)

    new_tile = pl.program_id(1) == 0
    pl.when(new_tile)(functools.partial(step, True))
    pl.when(jnp.logical_not(new_tile))(functools.partial(step, False))


def _ffn_up(xn, w_in, layer, side, *, bm, bn):
    m, k = xn.shape
    nblk, mblk = D_FF // bn, m // bm
    n_steps = nblk * mblk
    side_in, side_out, side_shapes = [], [], []
    for w, w_layer in side:
        _, r, c = w.shape
        rows = min(t for t in range(16, r + 1, 16) if r % t == 0 and r // t <= n_steps)
        last = r // rows - 1
        side_in.append(pl.BlockSpec(
            (None, rows, c), lambda j, i, w_layer=w_layer, last=last: (w_layer, jnp.minimum(j * mblk + i, last), 0)))
        side_out.append(pl.BlockSpec((rows, c), lambda j, i, last=last: (jnp.minimum(j * mblk + i, last), 0)))
        side_shapes.append(jax.ShapeDtypeStruct((r, c), BF16))
    res = pl.pallas_call(
        functools.partial(_ffn_up_kernel, n_side=len(side)),
        grid=(nblk, mblk),
        in_specs=[pl.BlockSpec((bm, k), lambda j, i: (i, 0)),
                  pl.BlockSpec((None, k, bn), lambda j, i: (layer, 0, j)),
                  pl.BlockSpec((None, k, bn), lambda j, i: (layer, 0, j + nblk))] + side_in,
        out_specs=[pl.BlockSpec((bm, bn), lambda j, i: (i, j))] + side_out,
        out_shape=[jax.ShapeDtypeStruct((m, D_FF), BF16)] + side_shapes,
        scratch_shapes=[pltpu.VMEM((k, bn), BF16), pltpu.VMEM((k, bn), BF16)],
        compiler_params=_params(2),
        name="ffn_up",
    )(xn, w_in, w_in, *[w for w, _ in side])
    return res[0], res[1:]


def _res_mm_norm_kernel(a_ref, w_ref, r_ref, g_ref, o_ref, xn_ref):
    bm = o_ref.shape[0]
    for rows in (slice(0, bm // 2), slice(bm // 2, bm)):
        o_ref[rows, :] = r_ref[rows, :] + jnp.dot(a_ref[rows, :], w_ref[...], preferred_element_type=F32)
        _norm_rows(o_ref, g_ref, xn_ref, rows)


def _res_mm_norm(a, w, layer, res, gain, *, bm):
    m, k = a.shape
    n = w.shape[2]
    row_blk = lambda i: (i, 0)
    return pl.pallas_call(
        _res_mm_norm_kernel,
        grid=(m // bm,),
        in_specs=[pl.BlockSpec((bm, k), row_blk),
                  pl.BlockSpec((None, k, n), lambda i: (layer, 0, 0)),
                  pl.BlockSpec((bm, n), row_blk),
                  _vec_spec(gain)],
        out_specs=[pl.BlockSpec((bm, n), row_blk), pl.BlockSpec((bm, n), row_blk)],
        out_shape=[jax.ShapeDtypeStruct((m, n), F32), jax.ShapeDtypeStruct((m, n), BF16)],
        compiler_params=_params(1),
        name="res_mm_norm",
    )(a, w, res, gain[0])


def _res_mm_kernel(a_ref, w_ref, r_ref, o_ref):
    o_ref[...] = r_ref[...] + jnp.dot(a_ref[...], w_ref[...], preferred_element_type=F32)


def _res_mm(a, w, layer, res, *, bm, bn):
    m, k = a.shape
    n = w.shape[2]
    return pl.pallas_call(
        _res_mm_kernel,
        grid=(m // bm, n // bn),
        in_specs=[pl.BlockSpec((bm, k), lambda i, j: (i, 0)),
                  _w_spec(k, bn, layer, 0),
                  pl.BlockSpec((bm, bn), lambda i, j: (i, j))],
        out_specs=pl.BlockSpec((bm, bn), lambda i, j: (i, j)),
        out_shape=jax.ShapeDtypeStruct((m, n), F32),
        compiler_params=_params(2),
        name="res_mm",
    )(a, w, res)


GLA_ROWS = 512


def _split2(x):
    hi = x.astype(BF16)
    lo = (x - hi.astype(F32)).astype(BF16)
    return hi, lo


def _gla_kernel(p_ref, a_ref, w2_ref, b_ref, hg_ref, o_ref, state_ref):
    c = GLA_CHUNK

    @pl.when(pl.program_id(1) == 0)
    def _():
        state_ref[...] = jnp.zeros_like(state_ref)

    row = lax.broadcasted_iota(jnp.int32, (c, c), 0)
    col = lax.broadcasted_iota(jnp.int32, (c, c), 1)
    causal = row >= col
    tril = causal.astype(BF16)
    head_gain = hg_ref[...]
    tn_dims = (((0,), (0,)), ((), ()))
    nt_dims = (((1,), (1,)), ((), ()))

    heads = range(GLA_HEADS)
    hk = [slice(h * GLA_HK, (h + 1) * GLA_HK) for h in heads]
    hv = [slice(h * GLA_HV, (h + 1) * GLA_HV) for h in heads]

    def rows_of(ci):
        return slice(ci * c, (ci + 1) * c)

    def gate_preact(ci):
        a_low = a_ref[rows_of(ci), :].astype(BF16)
        return jnp.dot(a_low, w2_ref[...], preferred_element_type=F32) + b_ref[...]

    def cum_log_decay(z):
        log_a = (jnp.minimum(z, 0.0) - jnp.log(1.0 + jnp.exp(-jnp.abs(z)))) / GLA_GATE_TAU
        return sum(jnp.dot(tril, part, preferred_element_type=F32) for part in _split2(log_a))

    def decayed_operands(ci, bcum):
        b_last = bcum[c - 1:c, :]
        q = p_ref[rows_of(ci), 0:GLA_DK]
        k = p_ref[rows_of(ci), GLA_DK:2 * GLA_DK]
        q_dec = ((q * (GLA_HK ** -0.5)) * jnp.exp(bcum)).astype(BF16)
        k_inv = (k * jnp.exp(-bcum)).astype(BF16)
        k_end = (k * jnp.exp(b_last - bcum)).astype(BF16)
        return q_dec, k_inv, k_end, jnp.exp(b_last)

    n_chunks = GLA_ROWS // c
    prepared = decayed_operands(0, cum_log_decay(gate_preact(0)))
    for ci in range(n_chunks):
        rows = rows_of(ci)
        q_dec, k_inv, k_end, chunk_decay = prepared
        has_next = ci + 1 < n_chunks
        if has_next:
            z_next = gate_preact(ci + 1)
        v_b = p_ref[rows, 2 * GLA_DK:2 * GLA_DK + GLA_DV].astype(BF16)
        states = [state_ref[h] for h in heads]
        o_inter = [jnp.dot(q_dec[:, hk[h]], states[h].astype(BF16), preferred_element_type=F32) for h in heads]
        kv = [lax.dot_general(k_end[:, hk[h]], v_b[:, hv[h]], tn_dims, preferred_element_type=F32)
              for h in heads]
        attn = [lax.dot_general(q_dec[:, hk[h]], k_inv[:, hk[h]], nt_dims, preferred_element_type=F32)
                for h in heads]
        if has_next:
            bcum_next = cum_log_decay(z_next)
        o_intra = [jnp.dot(jnp.where(causal, attn[h], 0.0).astype(BF16), v_b[:, hv[h]],
                           preferred_element_type=F32) for h in heads]
        if has_next:
            prepared = decayed_operands(ci + 1, bcum_next)
        for h in heads:
            decay = jnp.broadcast_to(chunk_decay[:, hk[h]], (LANES, GLA_HK)).T
            state_ref[h] = states[h] * jnp.concatenate([decay] * (GLA_HV // LANES), axis=1) + kv[h]
            o = o_intra[h] + o_inter[h]
            ms = jnp.mean(o * o, axis=-1, keepdims=True)
            o = (o * lax.rsqrt(ms + EPS)) * head_gain
            r = p_ref[rows, 2 * GLA_DK + GLA_DV + h * GLA_HV:2 * GLA_DK + GLA_DV + (h + 1) * GLA_HV]
            o_ref[rows, hv[h]] = (o * _silu(r)).astype(o_ref.dtype)


def _gla_core(proj, a_low, w2, b_alpha, head_gain):
    nblk = SEQ // GLA_ROWS
    return pl.pallas_call(
        _gla_kernel,
        grid=(BATCH, nblk),
        in_specs=[pl.BlockSpec((GLA_ROWS, GLA_MAIN), lambda b, n: (b * nblk + n, 0)),
                  pl.BlockSpec((GLA_ROWS, LANES), lambda b, n: (b * nblk + n, 0)),
                  pl.BlockSpec((LANES, GLA_DK), lambda b, n: (0, 0)),
                  _vec_spec(b_alpha), _vec_spec(head_gain)],
        out_specs=pl.BlockSpec((GLA_ROWS, GLA_DV), lambda b, n: (b * nblk + n, 0)),
        out_shape=jax.ShapeDtypeStruct((TOKENS, GLA_DV), BF16),
        scratch_shapes=[pltpu.VMEM((GLA_HEADS, GLA_HK, GLA_HV), F32)],
        compiler_params=_params(2),
        name="gla_core",
    )(proj, a_low, w2, b_alpha[0], head_gain[0])


ATT_TQ = 256
LOG2E = math.log2(math.e)


def _lane_tiles(x):
    return [x[:, i * LANES:(i + 1) * LANES] for i in range(x.shape[1] // LANES)]


def _attn_kernel(slope_ref, q_ref, k_ref, v_ref, lq1_ref, lk1_ref, lq2_ref, lk2_ref, hg_ref, o_ref,
                 qx_ref, kx_ref, *, lambda_init):
    tq, dh = ATT_TQ, DIFF_HEAD_DIM
    nt_dims = (((1,), (1,)), ((), ()))
    lam = (jnp.exp(jnp.sum(lq1_ref[...] * lk1_ref[...], axis=-1, keepdims=True))
           - jnp.exp(jnp.sum(lq2_ref[...] * lk2_ref[...], axis=-1, keepdims=True))
           + lambda_init)
    out_gain = hg_ref[...] * (1.0 - lambda_init)

    slope2 = slope_ref[0] * LOG2E
    s_hi = slope2.astype(BF16).astype(F32)
    s_mid = (slope2 - s_hi).astype(BF16).astype(F32)
    s_lo = (slope2 - s_hi - s_mid).astype(BF16).astype(F32)
    lane = lax.broadcasted_iota(jnp.int32, (1, LANES), 1)
    q_feat = jnp.where(lane < 2, s_hi, jnp.where(lane < 4, s_mid, jnp.where(lane < 6, s_lo, 0.0)))
    pos = lax.broadcasted_iota(jnp.int32, (SEQ, LANES), 0)
    lane_k = lax.broadcasted_iota(jnp.int32, (SEQ, LANES), 1)
    pos_lo = jnp.bitwise_and(pos, 255)
    pos_piece = jnp.where(jnp.bitwise_and(lane_k, 1) == 0, pos - pos_lo, pos_lo)
    k_feat = jnp.where(lane_k < 6, pos_piece, 0).astype(F32).astype(BF16)
    q_feat = jnp.broadcast_to(q_feat, (SEQ, LANES)).astype(BF16)
    for comp in range(2):
        qx_ref[comp, :, 0:dh] = q_ref[:, comp * dh:(comp + 1) * dh]
        qx_ref[comp, :, dh:2 * dh] = q_feat
        kx_ref[comp, :, 0:dh] = k_ref[:, comp * dh:(comp + 1) * dh]
        kx_ref[comp, :, dh:2 * dh] = k_feat

    row = lax.broadcasted_iota(jnp.int32, (tq, tq), 0)
    col = lax.broadcasted_iota(jnp.int32, (tq, tq), 1)
    causal_tiles = _lane_tiles(row >= col)

    def scores(qi):
        q0, n_keys = qi * tq, (qi + 1) * tq
        return [lax.dot_general(qx_ref[comp, q0:q0 + tq, :], kx_ref[comp, 0:n_keys, :],
                                nt_dims, preferred_element_type=F32) for comp in range(2)]

    def attend(qi, qk):
        q0, n_keys = qi * tq, (qi + 1) * tq
        v_blk = v_ref[0:n_keys, :]
        n_diag = len(causal_tiles)
        outs = []
        for comp in range(2):
            tiles = _lane_tiles(qk[comp])
            tiles = tiles[:-n_diag] + [jnp.where(keep, ti, MASK_VALUE)
                                       for keep, ti in zip(causal_tiles, tiles[-n_diag:])]
            m = jnp.max(functools.reduce(jnp.maximum, tiles), axis=-1, keepdims=True)
            p_tiles = [jnp.exp2(ti - m) for ti in tiles]
            l = jnp.sum(functools.reduce(jnp.add, p_tiles), axis=-1, keepdims=True)
            p = jnp.concatenate([pt.astype(BF16) for pt in p_tiles], axis=1)
            outs.append(jnp.dot(p, v_blk, preferred_element_type=F32) / l)
        o = outs[0] - lam * outs[1]
        ms = jnp.mean(o * o, axis=-1, keepdims=True)
        o = (o * lax.rsqrt(ms + EPS)) * out_gain
        o_ref[q0:q0 + tq, :] = o.astype(o_ref.dtype)

    order = list(reversed(range(SEQ // tq)))
    qk_next = scores(order[0])
    for n, qi in enumerate(order):
        qk = qk_next
        if n + 1 < len(order):
            qk_next = scores(order[n + 1])
        attend(qi, qk)


def _diff_attn_core(q, k, v, slopes, lq1, lk1, lq2, lk2, head_gain, lambda_init):
    def head_blk(first):
        return pl.BlockSpec((SEQ, DIFF_VDIM), lambda b, h: (b, first + h))

    (q, q_first), (k, k_first), (v, v_first) = q, k, v
    vecs = [lq1, lk1, lq2, lk2, head_gain]
    return pl.pallas_call(
        functools.partial(_attn_kernel, lambda_init=lambda_init),
        grid=(BATCH, DIFF_HEADS),
        in_specs=[pl.BlockSpec((1, 1, LANES), lambda b, h: (h, 0, 0)),
                  head_blk(q_first), head_blk(k_first), head_blk(v_first)] + [_vec_spec(p) for p in vecs],
        out_specs=head_blk(0),
        out_shape=jax.ShapeDtypeStruct((TOKENS, DIFF_V), BF16),
        scratch_shapes=[pltpu.VMEM((2, SEQ, 2 * DIFF_HEAD_DIM), BF16),
                        pltpu.VMEM((2, SEQ, 2 * DIFF_HEAD_DIM), BF16)],
        compiler_params=_params(2),
        name="diff_attn",
    )(slopes, q, k, v, *[p[0] for p in vecs])


CAST_BLOCK_BYTES = 8 * 1024 * 1024


def _cast_kernel(x_ref, o_ref):
    o_ref[...] = x_ref[...].astype(o_ref.dtype)


def _to_bf16(w, layer):
    _, r, c = w.shape
    bc = next((t for t in (2048, 1024) if c % t == 0), c)
    br = max(t for t in range(16, r + 1, 16) if r % t == 0 and t * bc * 4 <= CAST_BLOCK_BYTES)
    return pl.pallas_call(
        _cast_kernel,
        grid=(r // br, c // bc),
        in_specs=[pl.BlockSpec((None, br, bc), lambda i, j: (layer, i, j))],
        out_specs=pl.BlockSpec((br, bc), lambda i, j: (i, j)),
        out_shape=jax.ShapeDtypeStruct((r, c), BF16),
        compiler_params=_params(2),
        name="cast_bf16",
    )(w)


def kernel(x, gla_attn_norm, gla_w_in, gla_w_alpha2, gla_b_alpha, gla_head_norm, gla_w_out, kv_norm, w_kv, k_norm, diff_attn_norm, diff_w_q, diff_q_norm, diff_lambda_q1, diff_lambda_k1, diff_lambda_q2, diff_lambda_k2, diff_head_norm, diff_w_out, ffn_norm, ffn_w_in, ffn_w_out):
    xt = x.reshape(TOKENS, D_MODEL)
    pad = LANES - GLA_GATE_RANK
    slopes = (2.0 ** (-8.0 * jnp.arange(1, DIFF_HEADS + 1, dtype=F32) / DIFF_HEADS))
    slopes = jnp.broadcast_to(slopes[:, None, None], (DIFF_HEADS, 1, LANES))

    gla_in_t = jnp.swapaxes(gla_w_in, 1, 2)
    w_kv3 = w_kv[None]

    gla_in_b = {0: _to_bf16(gla_in_t, 0)}
    gla_out_b = {0: _to_bf16(gla_w_out, 0)}
    side_jobs = {
        0: [("ffn_out", ffn_w_out, 0), ("gla_in", gla_in_t, 1), ("gla_out", gla_w_out, 1)],
        1: [("ffn_out", ffn_w_out, 1), ("kv", w_kv3, 0), ("q", diff_w_q, 0), ("o", diff_w_out, 0)],
        2: [("ffn_out", ffn_w_out, 2), ("q", diff_w_q, 1), ("o", diff_w_out, 1)],
        3: [("ffn_out", ffn_w_out, 3)],
    }
    ready = {}

    k_shared = v_shared = None
    for l in range(DEPTH):
        if l < N_A_LAYERS:
            i = l
            w_in_b = gla_in_b[0][None] if i == 0 else ready["gla_in", i]
            w_out_b = gla_out_b[0][None] if i == 0 else ready["gla_out", i]
            w2 = jnp.pad(gla_w_alpha2[i], ((0, pad), (0, 0))).astype(BF16)
            proj, a_low = _norm_proj(xt, _vec(gla_attn_norm, i), w_in_b, 0, n=GLA_MAIN, bm=1024, bn=1024,
                                     out_dtype=F32, gate_rank=GLA_GATE_RANK, w_transposed=True)
            o = _gla_core(proj, a_low, w2, _vec(gla_b_alpha, i), _vec(gla_head_norm, i))
            xt, xn = _res_mm_norm(o, w_out_b, 0, xt, _vec(ffn_norm, l), bm=512)
        else:
            j = l - N_A_LAYERS
            lambda_init = 0.8 - 0.6 * math.exp(-0.3 * l)
            q_scale = DIFF_HEAD_DIM ** -0.5 * LOG2E
            if l == N_A_LAYERS:
                kvq = _kvq_proj(xt, _vec(kv_norm), _vec(diff_attn_norm, j), ready["kv", 0], ready["q", j],
                                _vec(k_norm), _vec(diff_q_norm, j), q_scale, bm=1024, bn=1024)
                k_shared, v_shared = (kvq, 0), (kvq, DIFF_HEADS)
                q = (kvq, 2 * DIFF_HEADS)
            else:
                q = (_norm_proj(xt, _vec(diff_attn_norm, j), ready["q", j], 0, n=DIFF_QK, bm=1024, bn=1024,
                                out_dtype=BF16, gn_gain=_vec(diff_q_norm, j), scale=q_scale), 0)
            o = _diff_attn_core(q, k_shared, v_shared, slopes, _vec(diff_lambda_q1, j), _vec(diff_lambda_k1, j),
                                _vec(diff_lambda_q2, j), _vec(diff_lambda_k2, j), _vec(diff_head_norm, j),
                                lambda_init)
            xt, xn = _res_mm_norm(o, ready["o", j], 0, xt, _vec(ffn_norm, l), bm=512)
        jobs = side_jobs[l]
        act, casted = _ffn_up(xn, ffn_w_in, l, [(w, idx) for _, w, idx in jobs], bm=1024, bn=512)
        for (name, _, idx), w_b in zip(jobs, casted):
            ready[name, idx] = w_b[None]
        xt = _res_mm(act, ready["ffn_out", l], 0, xt, bm=1024, bn=512)
    return xt.reshape(BATCH, SEQ, D_MODEL)
```

```python
import functools
import math

import jax
import jax.numpy as jnp
from jax import lax
from jax.experimental import pallas as pl
from jax.experimental.pallas import tpu as pltpu

D_MODEL = 2048
BATCH = 4
SEQ = 2048
DEPTH = 4
TOKENS = BATCH * SEQ
N_A_LAYERS = DEPTH // 2
N_B_LAYERS = DEPTH - N_A_LAYERS
EPS = 1e-6

GLA_HEADS = 4
GLA_DK = D_MODEL // 2
GLA_DV = D_MODEL
GLA_HK = GLA_DK // GLA_HEADS
GLA_HV = GLA_DV // GLA_HEADS
GLA_GATE_RANK = 16
GLA_GATE_TAU = 16.0
GLA_CHUNK = 64
GLA_MAIN = 2 * GLA_DK + 2 * GLA_DV

DIFF_HEAD_DIM = 128
DIFF_HEADS = D_MODEL // (2 * DIFF_HEAD_DIM)
DIFF_VDIM = 2 * DIFF_HEAD_DIM
DIFF_QK = DIFF_HEADS * 2 * DIFF_HEAD_DIM
DIFF_V = DIFF_HEADS * DIFF_VDIM

D_FF = -(-8 * D_MODEL // (3 * 256)) * 256

LANES = 128
MASK_VALUE = -1e30

BF16 = jnp.bfloat16
F32 = jnp.float32

VMEM_LIMIT = 56 * 1024 * 1024


def _params(n_axes):
    return pltpu.CompilerParams(dimension_semantics=("arbitrary",) * n_axes,
                                vmem_limit_bytes=VMEM_LIMIT)


def _silu(x):
    return x / (1.0 + jnp.exp(-x))


def _vec(param, idx=None):
    if idx is None:
        return param.reshape(1, 1, -1), 0
    return param.reshape(param.shape[0], 1, param.shape[1]), idx


def _vec_spec(vec):
    arr, idx = vec
    return pl.BlockSpec((None, 1, arr.shape[2]), lambda *_: (idx, 0, 0))


NORM_ROWS = 128


def _norm_rows(x_ref, g_ref, xn_ref, rows):
    gain = g_ref[...]
    for start in range(rows.start, rows.stop, NORM_ROWS):
        chunk = slice(start, start + NORM_ROWS)
        xv = x_ref[chunk, :]
        ms = jnp.mean(xv * xv, axis=-1, keepdims=True)
        xn_ref[chunk, :] = ((xv * lax.rsqrt(ms + EPS)) * gain).astype(BF16)


def _group_norm_store(y, gn_ref, o_ref, scale):
    gain = gn_ref[...]
    for g in range(y.shape[1] // LANES):
        yg = y[:, g * LANES:(g + 1) * LANES]
        ms = jnp.mean(yg * yg, axis=-1, keepdims=True)
        val = (yg * lax.rsqrt(ms + EPS)) * gain
        if scale != 1.0:
            val = val * scale
        o_ref[:, g * LANES:(g + 1) * LANES] = val.astype(o_ref.dtype)


NT_DIMS = (((1,), (1,)), ((), ()))


def _norm_proj_kernel(x_ref, g_ref, w_ref, *rest, group_norm, scale, gate_rank, w_transposed):
    rest = list(rest)
    w1_ref = rest.pop(0) if gate_rank else None
    gn_ref = rest.pop(0) if group_norm else None
    o_ref = rest.pop(0)
    a_ref = rest.pop(0) if gate_rank else None
    xn_ref = rest.pop(0)

    bm = x_ref.shape[0]
    j = pl.program_id(1)

    def project(w_blk, rows):
        if w_transposed:
            return lax.dot_general(xn_ref[rows, :], w_blk, NT_DIMS, preferred_element_type=F32)
        return jnp.dot(xn_ref[rows, :], w_blk, preferred_element_type=F32)

    def emit(rows):
        y = project(w_ref[...], rows)
        if group_norm:
            _group_norm_store(y, gn_ref, o_ref.at[rows, :], scale)
        else:
            o_ref[rows, :] = y.astype(o_ref.dtype)

    @pl.when(j == 0)
    def _():
        for rows in (slice(0, bm // 2), slice(bm // 2, bm)):
            _norm_rows(x_ref, g_ref, xn_ref, rows)
            emit(rows)
        if gate_rank:
            a_ref[:, :gate_rank] = project(w1_ref[...], slice(0, bm))
            a_ref[:, gate_rank:] = jnp.zeros((bm, a_ref.shape[1] - gate_rank), F32)

    @pl.when(j > 0)
    def _():
        emit(slice(0, bm))


def _w_spec(k, bn, layer, col_block0):
    return pl.BlockSpec((None, k, bn), lambda i, j: (layer, 0, j + col_block0))


def _norm_proj(x, gain, w, layer, *, n, bm, bn, out_dtype, col0=0, gate_rank=0, gn_gain=None, scale=1.0,
               w_transposed=False):
    m, k = x.shape
    group_norm = gn_gain is not None
    assert not gate_rank or w_transposed
    if w_transposed:
        w_spec = pl.BlockSpec((None, bn, k), lambda i, j: (layer, j + col0 // bn, 0))
    else:
        w_spec = _w_spec(k, bn, layer, col0 // bn)
    in_specs = [pl.BlockSpec((bm, k), lambda i, j: (i, 0)), _vec_spec(gain), w_spec]
    args = [x, gain[0], w]
    if gate_rank:
        in_specs.append(pl.BlockSpec((None, gate_rank, k), lambda i, j: (layer, (col0 + n) // gate_rank, 0)))
        args.append(w)
    if group_norm:
        in_specs.append(_vec_spec(gn_gain))
        args.append(gn_gain[0])
    out_shape = [jax.ShapeDtypeStruct((m, n), out_dtype)]
    out_specs = [pl.BlockSpec((bm, bn), lambda i, j: (i, j))]
    if gate_rank:
        out_shape.append(jax.ShapeDtypeStruct((m, LANES), F32))
        out_specs.append(pl.BlockSpec((bm, LANES), lambda i, j: (i, 0)))
    res = pl.pallas_call(
        functools.partial(_norm_proj_kernel, group_norm=group_norm, scale=scale, gate_rank=gate_rank,
                          w_transposed=w_transposed),
        grid=(m // bm, n // bn),
        in_specs=in_specs,
        out_specs=out_specs,
        out_shape=out_shape,
        scratch_shapes=[pltpu.VMEM((bm, k), BF16)],
        compiler_params=_params(2),
        name="norm_proj",
    )(*args)
    return res if gate_rank else res[0]


def _kvq_kernel(x_ref, gkv_ref, gq_ref, wkv_ref, wq_ref, kn_ref, qn_ref, o_ref, xkv_ref, xq_ref, *,
                k_blocks, kv_blocks, q_scale):
    j = pl.program_id(1)
    bm = x_ref.shape[0]

    @pl.when(j == 0)
    def _():
        for rows in (slice(0, bm // 2), slice(bm // 2, bm)):
            _norm_rows(x_ref, gkv_ref, xkv_ref, rows)
            _norm_rows(x_ref, gq_ref, xq_ref, rows)
            _group_norm_store(jnp.dot(xkv_ref[rows, :], wkv_ref[...], preferred_element_type=F32), kn_ref,
                              o_ref.at[rows, :], 1.0)

    @pl.when(jnp.logical_and(j > 0, j < k_blocks))
    def _():
        _group_norm_store(jnp.dot(xkv_ref[...], wkv_ref[...], preferred_element_type=F32), kn_ref, o_ref, 1.0)

    @pl.when(jnp.logical_and(j >= k_blocks, j < kv_blocks))
    def _():
        o_ref[...] = jnp.dot(xkv_ref[...], wkv_ref[...], preferred_element_type=F32).astype(o_ref.dtype)

    @pl.when(j >= kv_blocks)
    def _():
        _group_norm_store(jnp.dot(xq_ref[...], wq_ref[...], preferred_element_type=F32), qn_ref, o_ref, q_scale)


def _kvq_proj(x, kv_gain, q_gain, w_kv, w_q, k_norm, q_norm, q_scale, *, bm, bn):
    m, k = x.shape
    kv_blocks, q_blocks = w_kv.shape[2] // bn, w_q.shape[2] // bn
    return pl.pallas_call(
        functools.partial(_kvq_kernel, k_blocks=DIFF_QK // bn, kv_blocks=kv_blocks, q_scale=q_scale),
        grid=(m // bm, kv_blocks + q_blocks),
        in_specs=[pl.BlockSpec((bm, k), lambda i, j: (i, 0)), _vec_spec(kv_gain), _vec_spec(q_gain),
                  pl.BlockSpec((None, k, bn), lambda i, j: (0, 0, jnp.minimum(j, kv_blocks - 1))),
                  pl.BlockSpec((None, k, bn), lambda i, j: (0, 0, jnp.maximum(j - kv_blocks, 0))),
                  _vec_spec(k_norm), _vec_spec(q_norm)],
        out_specs=pl.BlockSpec((bm, bn), lambda i, j: (i, j)),
        out_shape=jax.ShapeDtypeStruct((m, (kv_blocks + q_blocks) * bn), BF16),
        scratch_shapes=[pltpu.VMEM((bm, k), BF16), pltpu.VMEM((bm, k), BF16)],
        compiler_params=_params(2),
        name="kvq_proj",
    )(x, kv_gain[0], q_gain[0], w_kv, w_q, k_norm[0], q_norm[0])


def _ffn_up_kernel(xn_ref, wg_ref, wu_ref, *rest, n_side):
    side_src = rest[:n_side]
    o_ref = rest[n_side]
    side_dst = rest[n_side + 1:2 * n_side + 1]
    wgb_ref, wub_ref = rest[2 * n_side + 1:]

    def step(round_weights):
        xn = xn_ref[...]
        half = o_ref.shape[1] // 2
        for cols in (slice(0, half), slice(half, 2 * half)):
            if round_weights:
                wgb_ref[:, cols] = wg_ref[:, cols].astype(BF16)
                wub_ref[:, cols] = wu_ref[:, cols].astype(BF16)
            g = jnp.dot(xn, wgb_ref[:, cols], preferred_element_type=F32)
            u = jnp.dot(xn, wub_ref[:, cols], preferred_element_type=F32)
            o_ref[:, cols] = (_silu(g) * u).astype(o_ref.dtype)

        for src, dst in zip(side_src, side_dst):
            dst[...] = src[...].astype(dst.dtype)

    new_tile = pl.program_id(1) == 0
    pl.when(new_tile)(functools.partial(step, True))
    pl.when(jnp.logical_not(new_tile))(functools.partial(step, False))


def _ffn_up(xn, w_in, layer, side, *, bm, bn):
    m, k = xn.shape
    nblk, mblk = D_FF // bn, m // bm
    n_steps = nblk * mblk
    side_in, side_out, side_shapes = [], [], []
    for w, w_layer in side:
        _, r, c = w.shape
        rows = min(t for t in range(16, r + 1, 16) if r % t == 0 and r // t <= n_steps)
        last = r // rows - 1
        side_in.append(pl.BlockSpec(
            (None, rows, c), lambda j, i, w_layer=w_layer, last=last: (w_layer, jnp.minimum(j * mblk + i, last), 0)))
        side_out.append(pl.BlockSpec((rows, c), lambda j, i, last=last: (jnp.minimum(j * mblk + i, last), 0)))
        side_shapes.append(jax.ShapeDtypeStruct((r, c), BF16))
    res = pl.pallas_call(
        functools.partial(_ffn_up_kernel, n_side=len(side)),
        grid=(nblk, mblk),
        in_specs=[pl.BlockSpec((bm, k), lambda j, i: (i, 0)),
                  pl.BlockSpec((None, k, bn), lambda j, i: (layer, 0, j)),
                  pl.BlockSpec((None, k, bn), lambda j, i: (layer, 0, j + nblk))] + side_in,
        out_specs=[pl.BlockSpec((bm, bn), lambda j, i: (i, j))] + side_out,
        out_shape=[jax.ShapeDtypeStruct((m, D_FF), BF16)] + side_shapes,
        scratch_shapes=[pltpu.VMEM((k, bn), BF16), pltpu.VMEM((k, bn), BF16)],
        compiler_params=_params(2),
        name="ffn_up",
    )(xn, w_in, w_in, *[w for w, _ in side])
    return res[0], res[1:]


W_ROUND_ROWS = 256


def _res_mm_norm_kernel(a_ref, w_ref, r_ref, g_ref, o_ref, xn_ref, wb_ref):
    @pl.when(pl.program_id(0) == 0)
    def _():
        def body(c, carry):
            rows = pl.ds(pl.multiple_of(c * W_ROUND_ROWS, W_ROUND_ROWS), W_ROUND_ROWS)
            wb_ref[rows, :] = w_ref[rows, :].astype(BF16)
            return carry

        lax.fori_loop(0, w_ref.shape[0] // W_ROUND_ROWS, body, 0)

    bm = o_ref.shape[0]
    for rows in (slice(0, bm // 2), slice(bm // 2, bm)):
        o_ref[rows, :] = r_ref[rows, :] + jnp.dot(a_ref[rows, :], wb_ref[...], preferred_element_type=F32)
        _norm_rows(o_ref, g_ref, xn_ref, rows)


def _res_mm_norm(a, w, layer, res, gain, *, bm):
    m, k = a.shape
    n = w.shape[2]
    row_blk = lambda i: (i, 0)
    return pl.pallas_call(
        _res_mm_norm_kernel,
        grid=(m // bm,),
        in_specs=[pl.BlockSpec((bm, k), row_blk),
                  pl.BlockSpec((None, k, n), lambda i: (layer, 0, 0), pipeline_mode=pl.Buffered(1)),
                  pl.BlockSpec((bm, n), row_blk),
                  _vec_spec(gain)],
        out_specs=[pl.BlockSpec((bm, n), row_blk), pl.BlockSpec((bm, n), row_blk)],
        out_shape=[jax.ShapeDtypeStruct((m, n), F32), jax.ShapeDtypeStruct((m, n), BF16)],
        scratch_shapes=[pltpu.VMEM((k, n), BF16)],
        compiler_params=_params(1),
        name="res_mm_norm",
    )(a, w, res, gain[0])


def _res_mm_kernel(a_ref, w_ref, r_ref, o_ref):
    o_ref[...] = r_ref[...] + jnp.dot(a_ref[...], w_ref[...], preferred_element_type=F32)


def _res_mm(a, w, layer, res, *, bm, bn):
    m, k = a.shape
    n = w.shape[2]
    return pl.pallas_call(
        _res_mm_kernel,
        grid=(m // bm, n // bn),
        in_specs=[pl.BlockSpec((bm, k), lambda i, j: (i, 0)),
                  _w_spec(k, bn, layer, 0),
                  pl.BlockSpec((bm, bn), lambda i, j: (i, j))],
        out_specs=pl.BlockSpec((bm, bn), lambda i, j: (i, j)),
        out_shape=jax.ShapeDtypeStruct((m, n), F32),
        compiler_params=_params(2),
        name="res_mm",
    )(a, w, res)


GLA_ROWS = 512


def _split2(x):
    hi = x.astype(BF16)
    lo = (x - hi.astype(F32)).astype(BF16)
    return hi, lo


def _gla_kernel(p_ref, a_ref, w2_ref, b_ref, hg_ref, o_ref, state_ref):
    c = GLA_CHUNK

    @pl.when(pl.program_id(1) == 0)
    def _():
        state_ref[...] = jnp.zeros_like(state_ref)

    row = lax.broadcasted_iota(jnp.int32, (c, c), 0)
    col = lax.broadcasted_iota(jnp.int32, (c, c), 1)
    causal = row >= col
    tril = causal.astype(BF16)
    head_gain = hg_ref[...]
    tn_dims = (((0,), (0,)), ((), ()))
    nt_dims = (((1,), (1,)), ((), ()))

    heads = range(GLA_HEADS)
    hk = [slice(h * GLA_HK, (h + 1) * GLA_HK) for h in heads]
    hv = [slice(h * GLA_HV, (h + 1) * GLA_HV) for h in heads]

    def rows_of(ci):
        return slice(ci * c, (ci + 1) * c)

    def gate_preact(ci):
        a_low = a_ref[rows_of(ci), :].astype(BF16)
        return jnp.dot(a_low, w2_ref[...], preferred_element_type=F32) + b_ref[...]

    def cum_log_decay(z):
        log_a = (jnp.minimum(z, 0.0) - jnp.log(1.0 + jnp.exp(-jnp.abs(z)))) / GLA_GATE_TAU
        return sum(jnp.dot(tril, part, preferred_element_type=F32) for part in _split2(log_a))

    def decayed_operands(ci, bcum):
        b_last = bcum[c - 1:c, :]
        q = p_ref[rows_of(ci), 0:GLA_DK]
        k = p_ref[rows_of(ci), GLA_DK:2 * GLA_DK]
        q_dec = ((q * (GLA_HK ** -0.5)) * jnp.exp(bcum)).astype(BF16)
        k_inv = (k * jnp.exp(-bcum)).astype(BF16)
        k_end = (k * jnp.exp(b_last - bcum)).astype(BF16)
        return q_dec, k_inv, k_end, jnp.exp(b_last)

    n_chunks = GLA_ROWS // c
    prepared = decayed_operands(0, cum_log_decay(gate_preact(0)))
    for ci in range(n_chunks):
        rows = rows_of(ci)
        q_dec, k_inv, k_end, chunk_decay = prepared
        has_next = ci + 1 < n_chunks
        if has_next:
            z_next = gate_preact(ci + 1)
        v_b = p_ref[rows, 2 * GLA_DK:2 * GLA_DK + GLA_DV].astype(BF16)
        states = [state_ref[h] for h in heads]
        o_inter = [jnp.dot(q_dec[:, hk[h]], states[h].astype(BF16), preferred_element_type=F32) for h in heads]
        kv = [lax.dot_general(k_end[:, hk[h]], v_b[:, hv[h]], tn_dims, preferred_element_type=F32)
              for h in heads]
        attn = [lax.dot_general(q_dec[:, hk[h]], k_inv[:, hk[h]], nt_dims, preferred_element_type=F32)
                for h in heads]
        if has_next:
            bcum_next = cum_log_decay(z_next)
        o_intra = [jnp.dot(jnp.where(causal, attn[h], 0.0).astype(BF16), v_b[:, hv[h]],
                           preferred_element_type=F32) for h in heads]
        if has_next:
            prepared = decayed_operands(ci + 1, bcum_next)
        for h in heads:
            decay = jnp.broadcast_to(chunk_decay[:, hk[h]], (LANES, GLA_HK)).T
            state_ref[h] = states[h] * jnp.concatenate([decay] * (GLA_HV // LANES), axis=1) + kv[h]
            o = o_intra[h] + o_inter[h]
            ms = jnp.mean(o * o, axis=-1, keepdims=True)
            o = (o * lax.rsqrt(ms + EPS)) * head_gain
            r = p_ref[rows, 2 * GLA_DK + GLA_DV + h * GLA_HV:2 * GLA_DK + GLA_DV + (h + 1) * GLA_HV]
            o_ref[rows, hv[h]] = (o * _silu(r)).astype(o_ref.dtype)


def _gla_core(proj, a_low, w2, b_alpha, head_gain):
    nblk = SEQ // GLA_ROWS
    return pl.pallas_call(
        _gla_kernel,
        grid=(BATCH, nblk),
        in_specs=[pl.BlockSpec((GLA_ROWS, GLA_MAIN), lambda b, n: (b * nblk + n, 0)),
                  pl.BlockSpec((GLA_ROWS, LANES), lambda b, n: (b * nblk + n, 0)),
                  pl.BlockSpec((LANES, GLA_DK), lambda b, n: (0, 0)),
                  _vec_spec(b_alpha), _vec_spec(head_gain)],
        out_specs=pl.BlockSpec((GLA_ROWS, GLA_DV), lambda b, n: (b * nblk + n, 0)),
        out_shape=jax.ShapeDtypeStruct((TOKENS, GLA_DV), BF16),
        scratch_shapes=[pltpu.VMEM((GLA_HEADS, GLA_HK, GLA_HV), F32)],
        compiler_params=_params(2),
        name="gla_core",
    )(proj, a_low, w2, b_alpha[0], head_gain[0])


ATT_TQ = 256
LOG2E = math.log2(math.e)


def _lane_tiles(x):
    return [x[:, i * LANES:(i + 1) * LANES] for i in range(x.shape[1] // LANES)]


def _attn_kernel(slope_ref, q_ref, k_ref, v_ref, lq1_ref, lk1_ref, lq2_ref, lk2_ref, hg_ref, o_ref,
                 qx_ref, kx_ref, *, lambda_init):
    tq, dh = ATT_TQ, DIFF_HEAD_DIM
    nt_dims = (((1,), (1,)), ((), ()))
    lam = (jnp.exp(jnp.sum(lq1_ref[...] * lk1_ref[...], axis=-1, keepdims=True))
           - jnp.exp(jnp.sum(lq2_ref[...] * lk2_ref[...], axis=-1, keepdims=True))
           + lambda_init)
    out_gain = hg_ref[...] * (1.0 - lambda_init)

    slope2 = slope_ref[0] * LOG2E
    s_hi = slope2.astype(BF16).astype(F32)
    s_mid = (slope2 - s_hi).astype(BF16).astype(F32)
    s_lo = (slope2 - s_hi - s_mid).astype(BF16).astype(F32)
    lane = lax.broadcasted_iota(jnp.int32, (1, LANES), 1)
    q_feat = jnp.where(lane < 2, s_hi, jnp.where(lane < 4, s_mid, jnp.where(lane < 6, s_lo, 0.0)))
    pos = lax.broadcasted_iota(jnp.int32, (SEQ, LANES), 0)
    lane_k = lax.broadcasted_iota(jnp.int32, (SEQ, LANES), 1)
    pos_lo = jnp.bitwise_and(pos, 255)
    pos_piece = jnp.where(jnp.bitwise_and(lane_k, 1) == 0, pos - pos_lo, pos_lo)
    k_feat = jnp.where(lane_k < 6, pos_piece, 0).astype(F32).astype(BF16)
    q_feat = jnp.broadcast_to(q_feat, (SEQ, LANES)).astype(BF16)
    for comp in range(2):
        qx_ref[comp, :, 0:dh] = q_ref[:, comp * dh:(comp + 1) * dh]
        qx_ref[comp, :, dh:2 * dh] = q_feat
        kx_ref[comp, :, 0:dh] = k_ref[:, comp * dh:(comp + 1) * dh]
        kx_ref[comp, :, dh:2 * dh] = k_feat

    row = lax.broadcasted_iota(jnp.int32, (tq, tq), 0)
    col = lax.broadcasted_iota(jnp.int32, (tq, tq), 1)
    causal_tiles = _lane_tiles(row >= col)

    def scores(qi):
        q0, n_keys = qi * tq, (qi + 1) * tq
        return [lax.dot_general(qx_ref[comp, q0:q0 + tq, :], kx_ref[comp, 0:n_keys, :],
                                nt_dims, preferred_element_type=F32) for comp in range(2)]

    def attend(qi, qk):
        q0, n_keys = qi * tq, (qi + 1) * tq
        v_blk = v_ref[0:n_keys, :]
        n_diag = len(causal_tiles)
        outs = []
        for comp in range(2):
            tiles = _lane_tiles(qk[comp])
            tiles = tiles[:-n_diag] + [jnp.where(keep, ti, MASK_VALUE)
                                       for keep, ti in zip(causal_tiles, tiles[-n_diag:])]
            m = jnp.max(functools.reduce(jnp.maximum, tiles), axis=-1, keepdims=True)
            p_tiles = [jnp.exp2(ti - m) for ti in tiles]
            l = jnp.sum(functools.reduce(jnp.add, p_tiles), axis=-1, keepdims=True)
            p = jnp.concatenate([pt.astype(BF16) for pt in p_tiles], axis=1)
            outs.append(jnp.dot(p, v_blk, preferred_element_type=F32) / l)
        o = outs[0] - lam * outs[1]
        ms = jnp.mean(o * o, axis=-1, keepdims=True)
        o = (o * lax.rsqrt(ms + EPS)) * out_gain
        o_ref[q0:q0 + tq, :] = o.astype(o_ref.dtype)

    order = list(reversed(range(SEQ // tq)))
    qk_next = scores(order[0])
    for n, qi in enumerate(order):
        qk = qk_next
        if n + 1 < len(order):
            qk_next = scores(order[n + 1])
        attend(qi, qk)


def _diff_attn_core(q, k, v, slopes, lq1, lk1, lq2, lk2, head_gain, lambda_init):
    def head_blk(first):
        return pl.BlockSpec((SEQ, DIFF_VDIM), lambda b, h: (b, first + h))

    (q, q_first), (k, k_first), (v, v_first) = q, k, v
    vecs = [lq1, lk1, lq2, lk2, head_gain]
    return pl.pallas_call(
        functools.partial(_attn_kernel, lambda_init=lambda_init),
        grid=(BATCH, DIFF_HEADS),
        in_specs=[pl.BlockSpec((1, 1, LANES), lambda b, h: (h, 0, 0)),
                  head_blk(q_first), head_blk(k_first), head_blk(v_first)] + [_vec_spec(p) for p in vecs],
        out_specs=head_blk(0),
        out_shape=jax.ShapeDtypeStruct((TOKENS, DIFF_V), BF16),
        scratch_shapes=[pltpu.VMEM((2, SEQ, 2 * DIFF_HEAD_DIM), BF16),
                        pltpu.VMEM((2, SEQ, 2 * DIFF_HEAD_DIM), BF16)],
        compiler_params=_params(2),
        name="diff_attn",
    )(slopes, q, k, v, *[p[0] for p in vecs])


CAST_BLOCK_BYTES = 8 * 1024 * 1024


def _cast_kernel(x_ref, o_ref):
    o_ref[...] = x_ref[...].astype(o_ref.dtype)


def _to_bf16(w, layer):
    _, r, c = w.shape
    bc = next((t for t in (2048, 1024) if c % t == 0), c)
    br = max(t for t in range(16, r + 1, 16) if r % t == 0 and t * bc * 4 <= CAST_BLOCK_BYTES)
    return pl.pallas_call(
        _cast_kernel,
        grid=(r // br, c // bc),
        in_specs=[pl.BlockSpec((None, br, bc), lambda i, j: (layer, i, j))],
        out_specs=pl.BlockSpec((br, bc), lambda i, j: (i, j)),
        out_shape=jax.ShapeDtypeStruct((r, c), BF16),
        compiler_params=_params(2),
        name="cast_bf16",
    )(w)


def kernel(x, gla_attn_norm, gla_w_in, gla_w_alpha2, gla_b_alpha, gla_head_norm, gla_w_out, kv_norm, w_kv, k_norm, diff_attn_norm, diff_w_q, diff_q_norm, diff_lambda_q1, diff_lambda_k1, diff_lambda_q2, diff_lambda_k2, diff_head_norm, diff_w_out, ffn_norm, ffn_w_in, ffn_w_out):
    xt = x.reshape(TOKENS, D_MODEL)
    pad = LANES - GLA_GATE_RANK
    slopes = (2.0 ** (-8.0 * jnp.arange(1, DIFF_HEADS + 1, dtype=F32) / DIFF_HEADS))
    slopes = jnp.broadcast_to(slopes[:, None, None], (DIFF_HEADS, 1, LANES))

    gla_in_t = jnp.swapaxes(gla_w_in, 1, 2)
    w_kv3 = w_kv[None]

    gla_in_b = {0: _to_bf16(gla_in_t, 0)}
    side_jobs = {
        0: [("ffn_out", ffn_w_out, 0), ("gla_in", gla_in_t, 1)],
        1: [("ffn_out", ffn_w_out, 1), ("kv", w_kv3, 0), ("q", diff_w_q, 0)],
        2: [("ffn_out", ffn_w_out, 2), ("q", diff_w_q, 1)],
        3: [("ffn_out", ffn_w_out, 3)],
    }
    ready = {}

    k_shared = v_shared = None
    for l in range(DEPTH):
        if l < N_A_LAYERS:
            i = l
            w_in_b = gla_in_b[0][None] if i == 0 else ready["gla_in", i]
            w2 = jnp.pad(gla_w_alpha2[i], ((0, pad), (0, 0))).astype(BF16)
            proj, a_low = _norm_proj(xt, _vec(gla_attn_norm, i), w_in_b, 0, n=GLA_MAIN, bm=1024, bn=1024,
                                     out_dtype=F32, gate_rank=GLA_GATE_RANK, w_transposed=True)
            o = _gla_core(proj, a_low, w2, _vec(gla_b_alpha, i), _vec(gla_head_norm, i))
            xt, xn = _res_mm_norm(o, gla_w_out, i, xt, _vec(ffn_norm, l), bm=512)
        else:
            j = l - N_A_LAYERS
            lambda_init = 0.8 - 0.6 * math.exp(-0.3 * l)
            q_scale = DIFF_HEAD_DIM ** -0.5 * LOG2E
            if l == N_A_LAYERS:
                kvq = _kvq_proj(xt, _vec(kv_norm), _vec(diff_attn_norm, j), ready["kv", 0], ready["q", j],
                                _vec(k_norm), _vec(diff_q_norm, j), q_scale, bm=1024, bn=1024)
                k_shared, v_shared = (kvq, 0), (kvq, DIFF_HEADS)
                q = (kvq, 2 * DIFF_HEADS)
            else:
                q = (_norm_proj(xt, _vec(diff_attn_norm, j), ready["q", j], 0, n=DIFF_QK, bm=1024, bn=1024,
                                out_dtype=BF16, gn_gain=_vec(diff_q_norm, j), scale=q_scale), 0)
            o = _diff_attn_core(q, k_shared, v_shared, slopes, _vec(diff_lambda_q1, j), _vec(diff_lambda_k1, j),
                                _vec(diff_lambda_q2, j), _vec(diff_lambda_k2, j), _vec(diff_head_norm, j),
                                lambda_init)
            xt, xn = _res_mm_norm(o, diff_w_out, j, xt, _vec(ffn_norm, l), bm=512)
        jobs = side_jobs[l]
        act, casted = _ffn_up(xn, ffn_w_in, l, [(w, idx) for _, w, idx in jobs], bm=1024, bn=512)
        for (name, _, idx), w_b in zip(jobs, casted):
            ready[name, idx] = w_b[None]
        xt = _res_mm(act, ready["ffn_out", l], 0, xt, bm=1024, bn=512)
    return xt.reshape(BATCH, SEQ, D_MODEL)
```
